```python
import jax
import jax.numpy as jnp
from jax import lax
import numpy as np

D_MODEL = 2048
BATCH = 4
SEQ = 2048
DEPTH = 2
DEC_BATCH = 32
DEC_SEQ = 32
PAST_LEN = 2048

CHUNK = 64
N_MEM = 256
MEM_HEADS = 4
MEM_HEAD_DIM = D_MODEL // MEM_HEADS
D_FF = 4 * D_MODEL
POOL_WIDTH = D_MODEL // 4
POOL_WINDOWS = (2, 4, 8, 16)
POOL_GROUPS = len(POOL_WINDOWS)
POOL_GROUP_DIM = POOL_WIDTH // POOL_GROUPS
POOL_HIST = max(POOL_WINDOWS) - 1
GMLP_WIDTH = D_MODEL // 4
GMLP_HEADS = 4
GMLP_HEAD_DIM = GMLP_WIDTH // GMLP_HEADS
GMLP_CHUNK = 128
MLSTM_WIDTH = D_MODEL - POOL_WIDTH - GMLP_WIDTH
MLSTM_HEADS = 8
MLSTM_HEAD_DIM = MLSTM_WIDTH // MLSTM_HEADS
MIX_WIDTH = POOL_WIDTH + GMLP_WIDTH + MLSTM_WIDTH
IN_COLS = POOL_WIDTH + 2 * GMLP_WIDTH + 4 * MLSTM_WIDTH + 2 * MLSTM_HEADS
EPS = 1e-6

kernel_name = "hybrid_pool_gmlp_mlstm_stream_step"


def rmsnorm(x, g):
    xf = x.astype(jnp.float32)
    y = xf * lax.rsqrt(jnp.mean(xf * xf, axis=-1, keepdims=True) + EPS)
    return (y * g.astype(jnp.float32)).astype(x.dtype)


def swiglu(h, w_in, w_out):
    gate, up = jnp.split(h @ w_in, 2, axis=-1)
    return (jax.nn.silu(gate) * up) @ w_out


def pool_mix(xp, hist, pos0, w_pool, scale):
    B, L, C = xp.shape
    full = jnp.concatenate([hist.astype(xp.dtype), xp], axis=1)
    cs = jnp.cumsum(full.astype(jnp.float32), axis=1)
    cs = jnp.pad(cs, ((0, 0), (1, 0), (0, 0)))
    ends = cs[:, POOL_HIST + 1:]
    pos = pos0 + jnp.arange(L)
    xf = xp.astype(jnp.float32)
    outs = []
    for g, w in enumerate(POOL_WINDOWS):
        sl = slice(g * POOL_GROUP_DIM, (g + 1) * POOL_GROUP_DIM)
        start = cs[:, POOL_HIST + 1 - w: POOL_HIST + 1 - w + L, sl]
        cnt = jnp.minimum(pos + 1, w).astype(jnp.float32)[None, :, None]
        outs.append((ends[..., sl] - start) / cnt - xf[..., sl])
    d = jnp.stack(outs, axis=2)
    y = jnp.einsum('blgc,gce->blge', d, w_pool.astype(jnp.float32)).reshape(B, L, C)
    y = y * scale.astype(jnp.float32)
    return y.astype(xp.dtype), full[:, -POOL_HIST:]


def gmlp_mix(u, v, v_gain, ws, bs):
    B, L, _ = u.shape
    Lc = min(L, GMLP_CHUNK)
    n = L // Lc
    vh = rmsnorm(v.reshape(B, L, GMLP_HEADS, GMLP_HEAD_DIM), v_gain.reshape(GMLP_HEADS, GMLP_HEAD_DIM))
    mask = jnp.tril(jnp.ones((Lc, Lc), dtype=bool))
    wsm = jnp.where(mask[None], ws[:, :Lc, :Lc], 0)
    vc = vh.reshape(B, n, Lc, GMLP_HEADS, GMLP_HEAD_DIM)
    s = jnp.einsum('hij,bnjhd->bnihd', wsm, vc) + bs[:, :Lc].T[None, None, :, :, None]
    out = u * s.reshape(B, L, GMLP_WIDTH).astype(u.dtype)
    return out, vh.reshape(B, L, GMLP_WIDTH)


def mlstm_chunk(carry, inp):
    C, n, m = carry
    q, k, v, ig, lf = inp
    L = q.shape[2]
    b = jnp.cumsum(lf, axis=-1)
    mask = jnp.tril(jnp.ones((L, L), dtype=bool))
    D = jnp.where(mask, b[..., :, None] - b[..., None, :] + ig[..., None, :], -jnp.inf)
    inter = b + m[..., None]
    m_row = jnp.maximum(inter, jnp.max(D, axis=-1))
    Dexp = jnp.exp(D - m_row[..., None])
    inter_w = jnp.exp(inter - m_row)
    S = jnp.einsum('bhid,bhjd->bhij', q, k) * Dexp
    num = jnp.einsum('bhij,bhjd->bhid', S, v) + inter_w[..., None] * jnp.einsum('bhid,bhde->bhie', q, C)
    den = jnp.sum(S, axis=-1) + inter_w * jnp.einsum('bhid,bhd->bhi', q, n)
    h = num / jnp.maximum(jnp.abs(den), jnp.exp(-m_row))[..., None]
    bL = b[..., -1]
    wlog = bL[..., None] - b + ig
    m_new = jnp.maximum(bL + m, jnp.max(wlog, axis=-1))
    decay = jnp.exp(bL + m - m_new)
    wt = jnp.exp(wlog - m_new[..., None])
    C_new = decay[..., None, None] * C + jnp.einsum('bht,bhtd,bhte->bhde', wt, k, v)
    n_new = decay[..., None] * n + jnp.einsum('bht,bhtd->bhd', wt, k)
    return (C_new, n_new, m_new), h


def mlstm_mix(q, k, v, o, ig, fg, state, norm_gain):
    B, L, _ = q.shape
    H, dh = MLSTM_HEADS, MLSTM_HEAD_DIM
    Lc = min(L, CHUNK)
    nc = L // Lc

    def heads(t):
        return t.astype(jnp.float32).reshape(B, nc, Lc, H, dh).transpose(1, 0, 3, 2, 4)

    def gates(t):
        return t.astype(jnp.float32).reshape(B, nc, Lc, H).transpose(1, 0, 3, 2)

    qs, ks, vs = heads(q), heads(k) * (dh ** -0.5), heads(v)
    igs = gates(ig)
    lfs = jax.nn.log_sigmoid(gates(fg))
    new_state, h = lax.scan(mlstm_chunk, state, (qs, ks, vs, igs, lfs))
    h = h.transpose(1, 0, 3, 2, 4).reshape(B, L, H, dh)
    h = rmsnorm(h, norm_gain.reshape(H, dh)).reshape(B, L, MLSTM_WIDTH)
    out = (h * jax.nn.sigmoid(o.astype(jnp.float32))).astype(q.dtype)
    return out, new_state


def mem_attend(h, mem_k, mem_v, w_q, w_o):
    B, L, _ = h.shape
    q = (h @ w_q).reshape(B, L, MEM_HEADS, MEM_HEAD_DIM).astype(jnp.float32)
    k = mem_k.reshape(B, N_MEM, MEM_HEADS, MEM_HEAD_DIM).astype(jnp.float32)
    v = mem_v.reshape(B, N_MEM, MEM_HEADS, MEM_HEAD_DIM).astype(jnp.float32)
    s = jnp.einsum('blhd,bmhd->bhlm', q, k) * (MEM_HEAD_DIM ** -0.5)
    p = jax.nn.softmax(s, axis=-1)
    o = jnp.einsum('bhlm,bmhd->blhd', p, v).reshape(B, L, D_MODEL).astype(h.dtype)
    return o @ w_o


def layer(x, pos0, mem_k, mem_v, pool_hist, mstate, p):
    x = x + 0.5 * swiglu(rmsnorm(x, p['g_ffn1']), p['w_ffn1_in'], p['w_ffn1_out'])
    proj = rmsnorm(x, p['g_mix']) @ p['w_in']
    o0 = POOL_WIDTH
    o1 = o0 + GMLP_WIDTH
    o2 = o1 + GMLP_WIDTH
    o3 = o2 + MLSTM_WIDTH
    o4 = o3 + MLSTM_WIDTH
    o5 = o4 + MLSTM_WIDTH
    o6 = o5 + MLSTM_WIDTH
    o7 = o6 + MLSTM_HEADS
    pool_y, pool_state = pool_mix(proj[..., :o0], pool_hist, pos0, p['pool_w'], p['pool_scale'])
    gmlp_y, v_rows = gmlp_mix(proj[..., o0:o1], proj[..., o1:o2], p['gmlp_v_gain'], p['gmlp_ws'], p['gmlp_bs'])
    ig = proj[..., o6:o7] + p['mlstm_i_bias']
    fg = proj[..., o7:] + p['mlstm_f_bias']
    mlstm_y, mstate = mlstm_mix(proj[..., o2:o3], proj[..., o3:o4], proj[..., o4:o5], proj[..., o5:o6],
                                ig, fg, mstate, p['mlstm_norm_gain'])
    x = x + jnp.concatenate([pool_y, gmlp_y, mlstm_y], axis=-1) @ p['w_out']
    x = x + mem_attend(rmsnorm(x, p['g_xattn']), mem_k, mem_v, p['w_mem_q'], p['w_mem_o'])
    x = x + 0.5 * swiglu(rmsnorm(x, p['g_ffn2']), p['w_ffn2_in'], p['w_ffn2_out'])
    return x, pool_state, v_rows, mstate


def setup_inputs(seed: int = 0) -> dict:
    key = jax.random.key(seed)
    ks = iter(jax.random.split(key, 48))

    def nrm(shape, scale=1.0):
        return jax.random.normal(next(ks), shape, dtype=jnp.float32) * scale

    def gain(shape):
        return 1.0 + nrm(shape, 0.05)

    H, dh = MLSTM_HEADS, MLSTM_HEAD_DIM
    return {
        'x_prompt': nrm((BATCH, SEQ, D_MODEL)),
        'x_sample': nrm((DEC_BATCH, DEC_SEQ, D_MODEL)),
        'mem_prompt': nrm((BATCH, N_MEM, D_MODEL)),
        'state_pool': nrm((DEPTH, DEC_BATCH, POOL_HIST, POOL_WIDTH)),
        'state_mlstm_C': nrm((DEPTH, DEC_BATCH, H, dh, dh), 0.5),
        'state_mlstm_n': nrm((DEPTH, DEC_BATCH, H, dh), 0.5),
        'state_mlstm_m': nrm((DEPTH, DEC_BATCH, H), 1.0),
        'cache_mem_k': nrm((DEPTH, DEC_BATCH, N_MEM, D_MODEL)),
        'cache_mem_v': nrm((DEPTH, DEC_BATCH, N_MEM, D_MODEL)),
        'g_ffn1': gain((DEPTH, D_MODEL)),
        'w_ffn1_in': nrm((DEPTH, D_MODEL, 2 * D_FF), D_MODEL ** -0.5),
        'w_ffn1_out': nrm((DEPTH, D_FF, D_MODEL), D_FF ** -0.5),
        'g_mix': gain((DEPTH, D_MODEL)),
        'w_in': nrm((DEPTH, D_MODEL, IN_COLS), D_MODEL ** -0.5),
        'pool_w': nrm((DEPTH, POOL_GROUPS, POOL_GROUP_DIM, POOL_GROUP_DIM), POOL_GROUP_DIM ** -0.5),
        'pool_scale': 0.5 + nrm((DEPTH, POOL_WIDTH), 0.1),
        'gmlp_v_gain': gain((DEPTH, GMLP_WIDTH)),
        'gmlp_ws': nrm((DEPTH, GMLP_HEADS, GMLP_CHUNK, GMLP_CHUNK), 0.5 * GMLP_CHUNK ** -0.5),
        'gmlp_bs': 1.0 + nrm((DEPTH, GMLP_HEADS, GMLP_CHUNK), 0.1),
        'mlstm_i_bias': nrm((DEPTH, H), 0.5),
        'mlstm_f_bias': 3.0 + nrm((DEPTH, H), 0.5),
        'mlstm_norm_gain': gain((DEPTH, MLSTM_WIDTH)),
        'w_out': nrm((DEPTH, MIX_WIDTH, D_MODEL), MIX_WIDTH ** -0.5),
        'g_xattn': gain((DEPTH, D_MODEL)),
        'g_mem': gain((DEPTH, D_MODEL)),
        'w_mem_q': nrm((DEPTH, D_MODEL, D_MODEL), D_MODEL ** -0.5),
        'w_mem_k': nrm((DEPTH, D_MODEL, D_MODEL), D_MODEL ** -0.5),
        'w_mem_v': nrm((DEPTH, D_MODEL, D_MODEL), D_MODEL ** -0.5),
        'w_mem_o': nrm((DEPTH, D_MODEL, D_MODEL), D_MODEL ** -0.5),
        'g_ffn2': gain((DEPTH, D_MODEL)),
        'w_ffn2_in': nrm((DEPTH, D_MODEL, 2 * D_FF), D_MODEL ** -0.5),
        'w_ffn2_out': nrm((DEPTH, D_FF, D_MODEL), D_FF ** -0.5),
        'g_final': gain((D_MODEL,)),
    }


def reference(x_prompt, x_sample, mem_prompt, state_pool, state_mlstm_C, state_mlstm_n, state_mlstm_m,
              cache_mem_k, cache_mem_v, g_ffn1, w_ffn1_in, w_ffn1_out, g_mix, w_in, pool_w, pool_scale,
              gmlp_v_gain, gmlp_ws, gmlp_bs, mlstm_i_bias, mlstm_f_bias, mlstm_norm_gain, w_out,
              g_xattn, g_mem, w_mem_q, w_mem_k, w_mem_v, w_mem_o, g_ffn2, w_ffn2_in, w_ffn2_out, g_final):
    H, dh = MLSTM_HEADS, MLSTM_HEAD_DIM
    B = x_prompt.shape[0]
    yp, ys = x_prompt, x_sample
    pool_p, pool_s, C_p, n_p, m_p, C_s, n_s, m_s, v_s, mk_p, mv_p = ([] for _ in range(11))
    for l in range(DEPTH):
        p = {
            'g_ffn1': g_ffn1[l], 'w_ffn1_in': w_ffn1_in[l], 'w_ffn1_out': w_ffn1_out[l],
            'g_mix': g_mix[l], 'w_in': w_in[l], 'pool_w': pool_w[l], 'pool_scale': pool_scale[l],
            'gmlp_v_gain': gmlp_v_gain[l], 'gmlp_ws': gmlp_ws[l], 'gmlp_bs': gmlp_bs[l],
            'mlstm_i_bias': mlstm_i_bias[l], 'mlstm_f_bias': mlstm_f_bias[l],
            'mlstm_norm_gain': mlstm_norm_gain[l], 'w_out': w_out[l],
            'g_xattn': g_xattn[l], 'w_mem_q': w_mem_q[l], 'w_mem_o': w_mem_o[l],
            'g_ffn2': g_ffn2[l], 'w_ffn2_in': w_ffn2_in[l], 'w_ffn2_out': w_ffn2_out[l],
        }
        mem_n = rmsnorm(mem_prompt, g_mem[l])
        mk = mem_n @ w_mem_k[l]
        mv = mem_n @ w_mem_v[l]
        hist0 = jnp.zeros((B, POOL_HIST, POOL_WIDTH), dtype=yp.dtype)
        st0 = (jnp.zeros((B, H, dh, dh), jnp.float32), jnp.zeros((B, H, dh), jnp.float32),
               jnp.zeros((B, H), jnp.float32))
        yp, ps, _, (Cn, nn_, mn) = layer(yp, 0, mk, mv, hist0, st0, p)
        pool_p.append(ps); C_p.append(Cn); n_p.append(nn_); m_p.append(mn); mk_p.append(mk); mv_p.append(mv)
        st = (state_mlstm_C[l].astype(jnp.float32), state_mlstm_n[l].astype(jnp.float32),
              state_mlstm_m[l].astype(jnp.float32))
        ys, ps2, vrow, (Cn2, nn2, mn2) = layer(ys, PAST_LEN, cache_mem_k[l], cache_mem_v[l], state_pool[l], st, p)
        pool_s.append(ps2); C_s.append(Cn2); n_s.append(nn2); m_s.append(mn2); v_s.append(vrow)
    y_prompt = rmsnorm(yp, g_final)
    y_sample = rmsnorm(ys, g_final)
    return (y_prompt, y_sample, jnp.stack(pool_p), jnp.stack(pool_s),
            jnp.stack(C_p), jnp.stack(n_p), jnp.stack(m_p),
            jnp.stack(C_s), jnp.stack(n_s), jnp.stack(m_s),
            jnp.stack(v_s), jnp.stack(mk_p), jnp.stack(mv_p))
```

```python
import functools

import jax
import jax.numpy as jnp
from jax import lax
from jax.experimental import pallas as pl
from jax.experimental.pallas import tpu as pltpu

F32 = jnp.float32
BF16 = jnp.bfloat16

EPS = 1e-6
POOL_WINDOWS = (2, 4, 8, 16)
POOL_PAD = 16
MEM_HEADS = 4
LANE = 128
MXU_DIM = 256
VMEM_LIMIT = 56 * 1024 * 1024
NEG_BIG = -1e30

TM = 1024
NORM_ROWS = 256


def _cparams(*sem):
    return pltpu.CompilerParams(dimension_semantics=sem, vmem_limit_bytes=VMEM_LIMIT)


def _rms(x, g):
    ms = jnp.mean(x * x, axis=-1, keepdims=True)
    return x * lax.rsqrt(ms + EPS) * g


def _norm_rows_to(x_ref, g_ref, h_ref):
    rows = x_ref.shape[0]

    def body(r, carry):
        sl = pl.ds(pl.multiple_of(r * NORM_ROWS, NORM_ROWS), NORM_ROWS)
        h_ref[sl, :] = _rms(x_ref[sl, :], g_ref[...]).astype(BF16)
        return carry

    lax.fori_loop(0, rows // NORM_ROWS, body, 0)


def _ffn_in_kernel(x_ref, g_ref, wg_ref, wu_ref, o_ref, h_ref):
    @pl.when(pl.program_id(1) == 0)
    def _():
        _norm_rows_to(x_ref, g_ref, h_ref)

    h = h_ref[...]
    gate = jnp.dot(h, wg_ref[...], preferred_element_type=F32)
    up = jnp.dot(h, wu_ref[...], preferred_element_type=F32)
    o_ref[...] = (gate * jax.nn.sigmoid(gate) * up).astype(BF16)


def _ffn_in(x, g, w, *, tf=512):
    t, d = x.shape
    f = w.shape[1] // 2
    nf = f // tf
    return pl.pallas_call(
        _ffn_in_kernel,
        out_shape=jax.ShapeDtypeStruct((t, f), BF16),
        grid=(t // TM, nf),
        in_specs=[
            pl.BlockSpec((TM, d), lambda i, j: (i, 0)),
            pl.BlockSpec((1, d), lambda i, j: (0, 0)),
            pl.BlockSpec((d, tf), lambda i, j: (0, j)),
            pl.BlockSpec((d, tf), lambda i, j: (0, j + nf)),
        ],
        out_specs=pl.BlockSpec((TM, tf), lambda i, j: (i, j)),
        scratch_shapes=[pltpu.VMEM((TM, d), BF16)],
        compiler_params=_cparams("parallel", "arbitrary"),
        name="ffn_in",
    )(x, g, w, w)


def _norm_mm_kernel(x_ref, g_ref, w_ref, *rest, has_extra):
    if has_extra:
        we_ref, o_ref, oe_ref, h_ref = rest
    else:
        o_ref, h_ref = rest

    @pl.when(pl.program_id(1) == 0)
    def _():
        _norm_rows_to(x_ref, g_ref, h_ref)
        if has_extra:
            oe_ref[...] = jnp.dot(h_ref[...], we_ref[...], preferred_element_type=F32)

    o_ref[...] = jnp.dot(h_ref[...], w_ref[...], preferred_element_type=F32).astype(o_ref.dtype)


def _norm_mm(x, g, w, *, tn, out_dtype, w_extra=None):
    t, d = x.shape
    n = w.shape[1]
    has_extra = w_extra is not None
    in_specs = [
        pl.BlockSpec((TM, d), lambda i, j: (i, 0)),
        pl.BlockSpec((1, d), lambda i, j: (0, 0)),
        pl.BlockSpec((d, tn), lambda i, j: (0, j)),
    ]
    out_shape = [jax.ShapeDtypeStruct((t, n), out_dtype)]
    out_specs = [pl.BlockSpec((TM, tn), lambda i, j: (i, j))]
    args = [x, g, w]
    if has_extra:
        ne = w_extra.shape[1]
        in_specs.append(pl.BlockSpec((d, ne), lambda i, j: (0, 0)))
        out_shape.append(jax.ShapeDtypeStruct((t, ne), F32))
        out_specs.append(pl.BlockSpec((TM, ne), lambda i, j: (i, 0)))
        args.append(w_extra)
    res = pl.pallas_call(
        functools.partial(_norm_mm_kernel, has_extra=has_extra),
        out_shape=out_shape,
        grid=(t // TM, n // tn),
        in_specs=in_specs,
        out_specs=out_specs,
        scratch_shapes=[pltpu.VMEM((TM, d), BF16)],
        compiler_params=_cparams("parallel", "arbitrary"),
        name="norm_mm",
    )(*args)
    return res if has_extra else res[0]


def _mm_res_kernel(a_ref, w_ref, r_ref, o_ref, *, scale):
    k = pl.program_id(2)
    part = scale * jnp.dot(a_ref[...], w_ref[...], preferred_element_type=F32)

    @pl.when(k == 0)
    def _():
        o_ref[...] = r_ref[...] + part

    @pl.when(k > 0)
    def _():
        o_ref[...] += part


def _mm_res(a, w, res, *, scale, tn=1024, tk=2048):
    t, kdim = a.shape
    n = w.shape[1]
    return pl.pallas_call(
        functools.partial(_mm_res_kernel, scale=scale),
        out_shape=jax.ShapeDtypeStruct((t, n), F32),
        grid=(t // TM, n // tn, kdim // tk),
        in_specs=[
            pl.BlockSpec((TM, tk), lambda i, j, k: (i, k)),
            pl.BlockSpec((tk, tn), lambda i, j, k: (k, j)),
            pl.BlockSpec((TM, tn), lambda i, j, k: (i, j)),
        ],
        out_specs=pl.BlockSpec((TM, tn), lambda i, j, k: (i, j)),
        compiler_params=_cparams("parallel", "parallel", "arbitrary"),
        name="mm_res",
    )(a, w, res)


def _pool_gmlp_kernel(xp_ref, u_ref, v_ref, hist_ref, pw_ref, ps_ref, vg_ref, ws_ref, bs_ref,
                      mix_ref, pst_ref, *rest, rows, pos0, want_v):
    if want_v:
        vrows_ref, full_ref, vpad_ref = rest
    else:
        full_ref, vpad_ref = rest
    r = pl.program_id(1)
    cw = xp_ref.shape[1]
    ng = len(POOL_WINDOWS)
    gd = cw // ng

    @pl.when(r == 0)
    def _():
        full_ref[0:POOL_PAD, :] = hist_ref[0]

    @pl.when(r > 0)
    def _():
        full_ref[0:POOL_PAD, :] = full_ref[rows:rows + POOL_PAD, :]

    x = xp_ref[...]
    full_ref[POOL_PAD:POOL_PAD + rows, :] = x
    pos = pos0 + r * rows + lax.broadcasted_iota(jnp.int32, (rows, 1), 0)
    for g, w in enumerate(POOL_WINDOWS):
        cs = slice(g * gd, (g + 1) * gd)
        acc = x[:, cs]
        for s in range(1, w):
            acc = acc + full_ref[POOL_PAD - s:POOL_PAD - s + rows, cs]
        cnt = jnp.minimum(pos + 1, w).astype(F32)
        dg = acc / cnt - x[:, cs]
        y = jnp.dot(dg.astype(BF16), pw_ref[g], preferred_element_type=F32)
        mix_ref[:, cs] = (y * ps_ref[:, cs]).astype(BF16)
    pst_ref[0] = full_ref[rows:rows + POOL_PAD, :]

    nh = ws_ref.shape[0]
    hd = v_ref.shape[1] // nh
    ck = ws_ref.shape[1]
    lc = min(rows, ck)
    tri = (lax.broadcasted_iota(jnp.int32, (lc, ck), 1)
           <= lax.broadcasted_iota(jnp.int32, (lc, ck), 0))
    if lc < ck:
        @pl.when(jnp.logical_and(pl.program_id(0) == 0, r == 0))
        def _():
            vpad_ref[...] = jnp.zeros_like(vpad_ref)
    for h in range(nh):
        hs = slice(h * hd, (h + 1) * hd)
        wsm = jnp.where(tri, ws_ref[h, 0:lc, :], 0.0).astype(BF16)
        bias = bs_ref[0:lc, hs]
        for c in range(rows // lc):
            rs = slice(c * lc, (c + 1) * lc)
            vh = _rms(v_ref[rs, hs], vg_ref[:, hs])
            if want_v:
                vrows_ref[rs, hs] = vh
            if lc < ck:
                vpad_ref[0:lc, :] = vh.astype(BF16)
                rhs = vpad_ref[...]
            else:
                rhs = vh.astype(BF16)
            s = jnp.dot(wsm, rhs, preferred_element_type=F32) + bias
            mix_ref[rs, cw + h * hd:cw + (h + 1) * hd] = (u_ref[rs, hs] * s).astype(BF16)


def _pool_gmlp(proj, hist, pool_w, pool_scale, v_gain, ws, bs_full, *, row0, nseq, seqlen, rows,
               pos0, col0, want_v):
    cw = pool_w.shape[0] * pool_w.shape[1]
    gw = v_gain.shape[1]
    nt = seqlen // rows
    rb0 = row0 // rows
    cb = col0 // cw

    def rowmap(off):
        return lambda b, r: (rb0 + b * nt + r, cb + off)

    const2 = lambda b, r: (0, 0)
    const3 = lambda b, r: (0, 0, 0)
    out_shape = [jax.ShapeDtypeStruct((nseq * seqlen, cw + gw), BF16),
                 jax.ShapeDtypeStruct((nseq, POOL_PAD, cw), F32)]
    out_specs = [pl.BlockSpec((rows, cw + gw), lambda b, r: (b * nt + r, 0)),
                 pl.BlockSpec((1, POOL_PAD, cw), lambda b, r: (b, 0, 0))]
    if want_v:
        out_shape.append(jax.ShapeDtypeStruct((nseq * seqlen, gw), F32))
        out_specs.append(pl.BlockSpec((rows, gw), lambda b, r: (b * nt + r, 0)))
    return pl.pallas_call(
        functools.partial(_pool_gmlp_kernel, rows=rows, pos0=pos0, want_v=want_v),
        out_shape=out_shape,
        grid=(nseq, nt),
        in_specs=[
            pl.BlockSpec((rows, cw), rowmap(0)),
            pl.BlockSpec((rows, gw), rowmap(1)),
            pl.BlockSpec((rows, gw), rowmap(2)),
            pl.BlockSpec((1, POOL_PAD, cw), lambda b, r: (b, 0, 0)),
            pl.BlockSpec(pool_w.shape, const3),
            pl.BlockSpec(pool_scale.shape, const2),
            pl.BlockSpec(v_gain.shape, const2),
            pl.BlockSpec(ws.shape, const3),
            pl.BlockSpec(bs_full.shape, const2),
        ],
        out_specs=out_specs,
        scratch_shapes=[pltpu.VMEM((rows + 2 * POOL_PAD, cw), F32),
                        pltpu.VMEM((ws.shape[1], gw // ws.shape[0]), BF16)],
        compiler_params=_cparams("arbitrary", "arbitrary"),
        name="pool_gmlp",
    )(proj, proj, proj, hist, pool_w, pool_scale, v_gain, ws, bs_full)


def _split3(x):
    hi = x.astype(BF16)
    r1 = x - hi.astype(F32)
    mid = r1.astype(BF16)
    lo = (r1 - mid.astype(F32)).astype(BF16)
    return hi, mid, lo


_NT = (((1,), (1,)), ((), ()))
_TN = (((0,), (0,)), ((), ()))


def _mlstm_kernel(q_ref, k_ref, v_ref, o_ref, gt_ref, gb_ref, ng_ref, c0_ref, n0_ref, m0_ref,
                  mix_ref, c_out, n_out, m_out, c_s, n_s, m_s, *pads, lq, nchunk):
    ck = LANE
    nh = c_s.shape[0]
    dh = c_s.shape[1]
    c = pl.program_id(1)

    @pl.when(c == 0)
    def _():
        c_s[...] = c0_ref[0]
        n_s[...] = n0_ref[0]
        m_s[...] = m0_ref[0]

    if lq < ck:
        kp_ref, vp_ref, gp_ref = pads

        @pl.when(jnp.logical_and(pl.program_id(0) == 0, c == 0))
        def _():
            kp_ref[...] = jnp.zeros_like(kp_ref)
            vp_ref[...] = jnp.zeros_like(vp_ref)
            gp_ref[...] = jnp.zeros_like(gp_ref)

        kp_ref[0:lq, :] = k_ref[...]
        vp_ref[0:lq, :] = v_ref[...]
        gp_ref[0:lq, :] = gt_ref[...]
        k_src, v_src, g_src = kp_ref, vp_ref, gp_ref
    else:
        k_src, v_src, g_src = k_ref, v_ref, gt_ref

    row = lax.broadcasted_iota(jnp.int32, (ck, ck), 0)
    col = lax.broadcasted_iota(jnp.int32, (ck, ck), 1)
    tri = col <= row
    valid = lax.broadcasted_iota(jnp.int32, (ck, 1), 0) < lq

    gts = g_src[...] + gb_ref[...]
    ig = gts[:, 0:LANE]
    lf = jax.nn.log_sigmoid(gts[:, LANE:2 * LANE])
    tri_b = jnp.where(tri, 1.0, 0.0).astype(BF16)
    b_all = sum(jnp.dot(tri_b, p, preferred_element_type=F32) for p in _split3(lf))
    r_all = ig - b_all
    m_all = m_s[...]
    inter_all = b_all + m_all
    bl_all = b_all[lq - 1:lq, :]
    wlog_all = bl_all + r_all
    m_new = jnp.maximum(bl_all + m_all,
                        jnp.max(jnp.where(valid, wlog_all, NEG_BIG), axis=0, keepdims=True))
    decay_all = jnp.exp(bl_all + m_all - m_new)
    wt_all = jnp.where(valid, jnp.exp(wlog_all - m_new), 0.0)

    sel_r = lax.broadcasted_iota(jnp.int32, (nh * ck, LANE), 0)
    sel_c = lax.broadcasted_iota(jnp.int32, (nh * ck, LANE), 1)
    sel = jnp.where(sel_c == lax.shift_right_logical(sel_r, ck.bit_length() - 1), 1.0, 0.0)
    sel = sel.astype(BF16)
    rrow = sum(lax.dot_general(sel, p, _NT, preferred_element_type=F32) for p in _split3(r_all))

    kscale = dh ** -0.5
    for h in range(nh):
        hs = slice(h * dh, (h + 1) * dh)
        qf = q_ref[:, hs]
        kf = k_src[:, hs] * kscale
        vf = v_src[:, hs]
        qb = qf.astype(BF16)
        kb = kf.astype(BF16)
        dm = b_all[0:lq, h:h + 1] + rrow[h * ck:h * ck + lq, :]
        dm = jnp.where(tri[0:lq, :], dm, NEG_BIG)
        inter = inter_all[0:lq, h:h + 1]
        m_row = jnp.maximum(inter, jnp.max(dm, axis=-1, keepdims=True))
        dexp = jnp.exp(dm - m_row)
        inter_w = jnp.exp(inter - m_row)
        s = lax.dot_general(qb, kb, _NT, preferred_element_type=F32) * dexp
        ch = c_s[h]
        nrow = n_s[h:h + 1, :]
        num = (jnp.dot(s.astype(BF16), vf.astype(BF16), preferred_element_type=F32)
               + inter_w * jnp.dot(qb, ch.astype(BF16), preferred_element_type=F32))
        den = (jnp.sum(s, axis=-1, keepdims=True)
               + inter_w * jnp.sum(qf * nrow, axis=-1, keepdims=True))
        hv = num / jnp.maximum(jnp.abs(den), jnp.exp(-m_row))
        hn = _rms(hv, ng_ref[:, hs])
        mix_ref[:, hs] = (hn * jax.nn.sigmoid(o_ref[:, hs])).astype(BF16)

        wt = wt_all[:, h:h + 1]
        decay = decay_all[:, h:h + 1]
        wv = (wt * vf).astype(BF16)
        c_s[h] = decay * ch + lax.dot_general(kb, wv, _TN, preferred_element_type=F32)
        n_s[h:h + 1, :] = decay * nrow + jnp.sum(wt * kf, axis=0, keepdims=True)

    m_s[...] = m_new

    @pl.when(c == nchunk - 1)
    def _():
        c_out[0] = c_s[...]
        n_out[0] = n_s[...]
        m_out[0] = m_s[...]


def _mlstm(proj, gates, gbias, ngain, c0, n0, m0, *, row0, nseq, seqlen):
    nh, dh = c0.shape[1], c0.shape[2]
    w = nh * dh
    lq = min(seqlen, LANE)
    nchunk = seqlen // lq
    rb0 = row0 // lq

    def rowmap(off):
        return lambda b, c: (rb0 + b * nchunk + c, off)

    state_map = lambda b, c: (b, 0, 0)
    pads = []
    if lq < LANE:
        pads = [pltpu.VMEM((LANE, w), F32), pltpu.VMEM((LANE, w), F32),
                pltpu.VMEM((LANE, 2 * LANE), F32)]
    return pl.pallas_call(
        functools.partial(_mlstm_kernel, lq=lq, nchunk=nchunk),
        out_shape=[jax.ShapeDtypeStruct((nseq * seqlen, w), BF16),
                   jax.ShapeDtypeStruct(c0.shape, F32),
                   jax.ShapeDtypeStruct(n0.shape, F32),
                   jax.ShapeDtypeStruct(m0.shape, F32)],
        grid=(nseq, nchunk),
        in_specs=[
            pl.BlockSpec((lq, w), rowmap(0)),
            pl.BlockSpec((lq, w), rowmap(1)),
            pl.BlockSpec((lq, w), rowmap(2)),
            pl.BlockSpec((lq, w), rowmap(3)),
            pl.BlockSpec((lq, 2 * LANE), rowmap(0)),
            pl.BlockSpec((1, 2 * LANE), lambda b, c: (0, 0)),
            pl.BlockSpec((1, w), lambda b, c: (0, 0)),
            pl.BlockSpec((1, nh, dh, dh), lambda b, c: (b, 0, 0, 0)),
            pl.BlockSpec((1, nh, dh), state_map),
            pl.BlockSpec((1, 1, LANE), state_map),
        ],
        out_specs=[
            pl.BlockSpec((lq, w), lambda b, c: (b * nchunk + c, 0)),
            pl.BlockSpec((1, nh, dh, dh), lambda b, c: (b, 0, 0, 0)),
            pl.BlockSpec((1, nh, dh), state_map),
            pl.BlockSpec((1, 1, LANE), state_map),
        ],
        scratch_shapes=[pltpu.VMEM((nh, dh, dh), F32), pltpu.VMEM((nh, dh), F32),
                        pltpu.VMEM((1, LANE), F32)] + pads,
        compiler_params=_cparams("arbitrary", "arbitrary"),
        name="mlstm",
    )(proj, proj, proj, proj, gates, gbias, ngain, c0, n0, m0)


def _attn_kernel(q_ref, k_ref, v_ref, o_ref):
    d = q_ref.shape[1]
    hd = d // MEM_HEADS
    scale = hd ** -0.5
    for h in range(MEM_HEADS):
        hs = slice(h * hd, (h + 1) * hd)
        kb = k_ref[0, :, hs].astype(BF16)
        vb = v_ref[0, :, hs].astype(BF16)
        s = lax.dot_general(q_ref[:, hs], kb, _NT, preferred_element_type=F32) * scale
        p = jnp.exp(s - jnp.max(s, axis=-1, keepdims=True))
        p = p / jnp.sum(p, axis=-1, keepdims=True)
        o_ref[:, hs] = jnp.dot(p.astype(BF16), vb, preferred_element_type=F32).astype(BF16)


def _attn(q, mem_k, mem_v, *, row0, nseq, seqlen, rows):
    d = q.shape[1]
    nm = mem_k.shape[1]
    nt = seqlen // rows
    rb0 = row0 // rows
    return pl.pallas_call(
        _attn_kernel,
        out_shape=jax.ShapeDtypeStruct((nseq * seqlen, d), BF16),
        grid=(nseq, nt),
        in_specs=[
            pl.BlockSpec((rows, d), lambda b, r: (rb0 + b * nt + r, 0)),
            pl.BlockSpec((1, nm, d), lambda b, r: (b, 0, 0)),
            pl.BlockSpec((1, nm, d), lambda b, r: (b, 0, 0)),
        ],
        out_specs=pl.BlockSpec((rows, d), lambda b, r: (b * nt + r, 0)),
        compiler_params=_cparams("parallel", "arbitrary"),
        name="mem_attn",
    )(q, mem_k, mem_v)


def _final_norm_kernel(x_ref, g_ref, o_ref):
    o_ref[...] = _rms(x_ref[...], g_ref[...])


def _final_norm(x, g, *, row0, nrows, rows=512):
    d = x.shape[1]
    rb0 = row0 // rows
    return pl.pallas_call(
        _final_norm_kernel,
        out_shape=jax.ShapeDtypeStruct((nrows, d), F32),
        grid=(nrows // rows,),
        in_specs=[pl.BlockSpec((rows, d), lambda i: (rb0 + i, 0)),
                  pl.BlockSpec((1, d), lambda i: (0, 0))],
        out_specs=pl.BlockSpec((rows, d), lambda i: (i, 0)),
        compiler_params=_cparams("parallel"),
        name="final_norm",
    )(x, g)


def kernel(x_prompt, x_sample, mem_prompt, state_pool, state_mlstm_C, state_mlstm_n, state_mlstm_m, cache_mem_k, cache_mem_v, g_ffn1, w_ffn1_in, w_ffn1_out, g_mix, w_in, pool_w, pool_scale, gmlp_v_gain, gmlp_ws, gmlp_bs, mlstm_i_bias, mlstm_f_bias, mlstm_norm_gain, w_out, g_xattn, g_mem, w_mem_q, w_mem_k, w_mem_v, w_mem_o, g_ffn2, w_ffn2_in, w_ffn2_out, g_final):
    nb, seq, d = x_prompt.shape
    db, dseq, _ = x_sample.shape
    depth = g_ffn1.shape[0]
    nh, dh = state_mlstm_C.shape[2], state_mlstm_C.shape[3]
    mw = nh * dh
    pw = pool_w.shape[1] * pool_w.shape[2]
    gh = gmlp_ws.shape[1]
    gw = gmlp_v_gain.shape[1]
    ghd = gw // gh
    nmem = mem_prompt.shape[1]
    p_rows = nb * seq
    s_rows = db * dseq
    past = seq

    x = jnp.concatenate([x_prompt.reshape(p_rows, d), x_sample.reshape(s_rows, d)], axis=0)
    mem = mem_prompt.reshape(nb * nmem, d)

    outs = {k: [] for k in ("pool_p", "pool_s", "c_p", "n_p", "m_p", "c_s", "n_s", "m_s", "v_s",
                            "mk", "mv")}
    o0 = pw
    o1 = o0 + gw
    o2 = o1 + gw
    o3 = o2 + mw
    o4 = o3 + mw
    o5 = o4 + mw
    o6 = o5 + mw
    o7 = o6 + nh
    for l in range(depth):
        row = lambda v: v[l].reshape(1, -1)
        act = _ffn_in(x, row(g_ffn1), w_ffn1_in[l].astype(BF16))
        x = _mm_res(act, w_ffn1_out[l].astype(BF16), x, scale=0.5)

        wl = w_in[l]
        w_main = jnp.concatenate([wl[:, o2:o6], wl[:, 0:o2]], axis=1).astype(BF16)
        zpad = jnp.zeros((d, LANE - nh), F32)
        w_gate = jnp.concatenate([wl[:, o6:o7], zpad, wl[:, o7:], zpad], axis=1).astype(BF16)
        proj, gates = _norm_mm(x, row(g_mix), w_main, tn=512, out_dtype=F32, w_extra=w_gate)
        bpad = jnp.zeros((LANE - nh,), F32)
        gbias = jnp.concatenate([mlstm_i_bias[l], bpad, mlstm_f_bias[l], bpad]).reshape(1, -1)

        pwl = pool_w[l].astype(BF16)
        bs_full = jnp.repeat(gmlp_bs[l].T, ghd, axis=1)
        hist_p = jnp.zeros((nb, POOL_PAD, pw), F32)
        hist_s = jnp.pad(state_pool[l], ((0, 0), (POOL_PAD - state_pool.shape[2], 0), (0, 0)))
        common = (pwl, row(pool_scale), row(gmlp_v_gain), gmlp_ws[l], bs_full)
        mix_a_p, pst_p = _pool_gmlp(proj, hist_p, *common, row0=0, nseq=nb, seqlen=seq, rows=512,
                                    pos0=0, col0=4 * mw, want_v=False)
        mix_a_s, pst_s, v_rows = _pool_gmlp(proj, hist_s, *common, row0=p_rows, nseq=db,
                                            seqlen=dseq, rows=dseq, pos0=past, col0=4 * mw,
                                            want_v=True)
        npool = state_pool.shape[2]
        outs["pool_p"].append(pst_p[:, POOL_PAD - npool:])
        outs["pool_s"].append(pst_s[:, POOL_PAD - npool:])
        outs["v_s"].append(v_rows.reshape(db, dseq, gw))

        mpad = lambda m: jnp.pad(m, ((0, 0), (0, LANE - nh))).reshape(m.shape[0], 1, LANE)
        ngain = row(mlstm_norm_gain)
        mix_b_p, c_p, n_p, m_p = _mlstm(
            proj, gates, gbias, ngain, jnp.zeros((nb, nh, dh, dh), F32),
            jnp.zeros((nb, nh, dh), F32), jnp.zeros((nb, 1, LANE), F32),
            row0=0, nseq=nb, seqlen=seq)
        mix_b_s, c_s, n_s, m_s = _mlstm(
            proj, gates, gbias, ngain, state_mlstm_C[l].astype(F32), state_mlstm_n[l].astype(F32),
            mpad(state_mlstm_m[l].astype(F32)), row0=p_rows, nseq=db, seqlen=dseq)
        outs["c_p"].append(c_p)
        outs["n_p"].append(n_p)
        outs["m_p"].append(m_p[:, 0, :nh])
        outs["c_s"].append(c_s)
        outs["n_s"].append(n_s)
        outs["m_s"].append(m_s[:, 0, :nh])

        mix = jnp.concatenate([jnp.concatenate([mix_a_p, mix_b_p], axis=1),
                               jnp.concatenate([mix_a_s, mix_b_s], axis=1)], axis=0)
        x = _mm_res(mix, w_out[l].astype(BF16), x, scale=1.0)

        w_kv = jnp.concatenate([w_mem_k[l], w_mem_v[l]], axis=1).astype(BF16)
        mkv = _norm_mm(mem, row(g_mem), w_kv, tn=1024, out_dtype=F32)
        mk = mkv[:, :d].reshape(nb, nmem, d)
        mv = mkv[:, d:].reshape(nb, nmem, d)
        outs["mk"].append(mk)
        outs["mv"].append(mv)
        q = _norm_mm(x, row(g_xattn), w_mem_q[l].astype(BF16), tn=1024, out_dtype=BF16)
        att_p = _attn(q, mk, mv, row0=0, nseq=nb, seqlen=seq, rows=1024)
        att_s = _attn(q, cache_mem_k[l], cache_mem_v[l], row0=p_rows, nseq=db, seqlen=dseq,
                      rows=dseq)
        att = jnp.concatenate([att_p, att_s], axis=0)
        x = _mm_res(att, w_mem_o[l].astype(BF16), x, scale=1.0)

        act = _ffn_in(x, row(g_ffn2), w_ffn2_in[l].astype(BF16))
        x = _mm_res(act, w_ffn2_out[l].astype(BF16), x, scale=0.5)

    gfin = g_final.reshape(1, d)
    y_prompt = _final_norm(x, gfin, row0=0, nrows=p_rows).reshape(nb, seq, d)
    y_sample = _final_norm(x, gfin, row0=p_rows, nrows=s_rows).reshape(db, dseq, d)
    st = lambda k: jnp.stack(outs[k])
    return (y_prompt, y_sample, st("pool_p"), st("pool_s"), st("c_p"), st("n_p"), st("m_p"),
            st("c_s"), st("n_s"), st("m_s"), st("v_s"), st("mk"), st("mv"))
```

```python
import functools

import jax
import jax.numpy as jnp
from jax import lax
from jax.experimental import pallas as pl
from jax.experimental.pallas import tpu as pltpu

F32 = jnp.float32
BF16 = jnp.bfloat16

EPS = 1e-6
POOL_WINDOWS = (2, 4, 8, 16)
POOL_PAD = 16
MEM_HEADS = 4
LANE = 128
VMEM_LIMIT = 56 * 1024 * 1024
NEG_BIG = -1e30

TM = 1024
NORM_ROWS = 256


def _cparams(*sem):
    return pltpu.CompilerParams(dimension_semantics=sem, vmem_limit_bytes=VMEM_LIMIT)


def _rms(x, g):
    ms = jnp.mean(x * x, axis=-1, keepdims=True)
    return x * lax.rsqrt(ms + EPS) * g


def _norm_rows_to(x_ref, g_ref, h_ref):
    rows = x_ref.shape[0]

    def body(r, carry):
        sl = pl.ds(pl.multiple_of(r * NORM_ROWS, NORM_ROWS), NORM_ROWS)
        h_ref[sl, :] = _rms(x_ref[sl, :], g_ref[...]).astype(BF16)
        return carry

    lax.fori_loop(0, rows // NORM_ROWS, body, 0)


def _layer_spec(shape, l):
    zeros = (0,) * len(shape)
    return pl.BlockSpec((None,) + tuple(shape), lambda *_: (l,) + zeros)


def _ffn_in_kernel(x_ref, g_ref, wg_ref, wu_ref, o_ref, h_ref):
    @pl.when(pl.program_id(1) == 0)
    def _():
        _norm_rows_to(x_ref, g_ref, h_ref)

    h = h_ref[...]
    gate = jnp.dot(h, wg_ref[...], preferred_element_type=F32)
    up = jnp.dot(h, wu_ref[...], preferred_element_type=F32)
    o_ref[...] = (gate * jax.nn.sigmoid(gate) * up).astype(BF16)


def _ffn_in(x, g, w, l, *, tf=512):
    t, d = x.shape
    f = w.shape[2] // 2
    nf = f // tf
    return pl.pallas_call(
        _ffn_in_kernel,
        out_shape=jax.ShapeDtypeStruct((t, f), BF16),
        grid=(t // TM, nf),
        in_specs=[
            pl.BlockSpec((TM, d), lambda i, j: (i, 0)),
            _layer_spec((1, d), l),
            pl.BlockSpec((None, d, tf), lambda i, j: (l, 0, j)),
            pl.BlockSpec((None, d, tf), lambda i, j: (l, 0, j + nf)),
        ],
        out_specs=pl.BlockSpec((TM, tf), lambda i, j: (i, j)),
        scratch_shapes=[pltpu.VMEM((TM, d), BF16)],
        compiler_params=_cparams("parallel", "arbitrary"),
        name="ffn_in",
    )(x, g, w, w)


def _norm_mm_kernel(x_ref, g_ref, w_ref, *rest, has_extra):
    if has_extra:
        we_ref, o_ref, oe_ref, h_ref = rest
    else:
        o_ref, h_ref = rest

    @pl.when(pl.program_id(1) == 0)
    def _():
        _norm_rows_to(x_ref, g_ref, h_ref)
        if has_extra:
            oe_ref[...] = jnp.dot(h_ref[...], we_ref[...], preferred_element_type=F32)

    o_ref[...] = jnp.dot(h_ref[...], w_ref[...], preferred_element_type=F32).astype(o_ref.dtype)


def _norm_mm(x, g, w, l, *, tn, n_tiles, out_dtype, tile_shift=0, w_extra=None):
    t, d = x.shape
    has_extra = w_extra is not None
    in_specs = [
        pl.BlockSpec((TM, d), lambda i, j: (i, 0)),
        _layer_spec((1, d), l),
        pl.BlockSpec((None, d, tn), lambda i, j: (l, 0, j)),
    ]
    out_shape = [jax.ShapeDtypeStruct((t, n_tiles * tn), out_dtype)]
    out_specs = [pl.BlockSpec((TM, tn), lambda i, j: (i, (j + tile_shift) % n_tiles))]
    args = [x, g, w]
    if has_extra:
        ne = w_extra.shape[2]
        in_specs.append(_layer_spec((d, ne), l))
        out_shape.append(jax.ShapeDtypeStruct((t, ne), F32))
        out_specs.append(pl.BlockSpec((TM, ne), lambda i, j: (i, 0)))
        args.append(w_extra)
    res = pl.pallas_call(
        functools.partial(_norm_mm_kernel, has_extra=has_extra),
        out_shape=out_shape,
        grid=(t // TM, n_tiles),
        in_specs=in_specs,
        out_specs=out_specs,
        scratch_shapes=[pltpu.VMEM((TM, d), BF16)],
        compiler_params=_cparams("parallel", "arbitrary"),
        name="norm_mm",
    )(*args)
    return res if has_extra else res[0]


def _mem_kv_kernel(x_ref, g_ref, wk_ref, wv_ref, ok_ref, ov_ref, h_ref):
    @pl.when(pl.program_id(1) == 0)
    def _():
        _norm_rows_to(x_ref, g_ref, h_ref)

    h = h_ref[...]
    ok_ref[...] = jnp.dot(h, wk_ref[...], preferred_element_type=F32)
    ov_ref[...] = jnp.dot(h, wv_ref[...], preferred_element_type=F32)


def _mem_kv(mem, g, wk, wv, *, tn=512):
    rows, d = mem.shape
    nl = g.shape[0]
    out = jax.ShapeDtypeStruct((nl, rows, d), F32)
    w_spec = pl.BlockSpec((None, d, tn), lambda ll, j: (ll, 0, j))
    o_spec = pl.BlockSpec((None, rows, tn), lambda ll, j: (ll, 0, j))
    return pl.pallas_call(
        _mem_kv_kernel,
        out_shape=[out, out],
        grid=(nl, d // tn),
        in_specs=[pl.BlockSpec((rows, d), lambda ll, j: (0, 0)),
                  pl.BlockSpec((None, 1, d), lambda ll, j: (ll, 0, 0)),
                  w_spec, w_spec],
        out_specs=[o_spec, o_spec],
        scratch_shapes=[pltpu.VMEM((rows, d), BF16)],
        compiler_params=_cparams("arbitrary", "arbitrary"),
        name="mem_kv",
    )(mem, g, wk, wv)


def _mm_res_kernel(a_ref, w_ref, r_ref, o_ref, *, scale):
    k = pl.program_id(2)
    part = scale * jnp.dot(a_ref[...], w_ref[...], preferred_element_type=F32)

    @pl.when(k == 0)
    def _():
        o_ref[...] = r_ref[...] + part

    @pl.when(k > 0)
    def _():
        o_ref[...] += part


def _mm_res(a, w, l, res, *, scale, tn=1024, tk=2048):
    t, kdim = a.shape
    n = w.shape[2]
    return pl.pallas_call(
        functools.partial(_mm_res_kernel, scale=scale),
        out_shape=jax.ShapeDtypeStruct((t, n), F32),
        grid=(t // TM, n // tn, kdim // tk),
        in_specs=[
            pl.BlockSpec((TM, tk), lambda i, j, k: (i, k)),
            pl.BlockSpec((None, tk, tn), lambda i, j, k: (l, k, j)),
            pl.BlockSpec((TM, tn), lambda i, j, k: (i, j)),
        ],
        out_specs=pl.BlockSpec((TM, tn), lambda i, j, k: (i, j)),
        compiler_params=_cparams("parallel", "parallel", "arbitrary"),
        name="mm_res",
    )(a, w, res)


def _mm_res_groups_kernel(*refs, n_parts, n_prompt_tiles):
    ap = refs[0:n_parts]
    asm = refs[n_parts:2 * n_parts]
    w_ref, r_ref, o_ref = refs[2 * n_parts:]
    i = pl.program_id(0)

    def run(parts):
        acc = r_ref[...]
        k0 = 0
        for p in parts:
            kw = p.shape[1]
            acc = acc + jnp.dot(p[...], w_ref[k0:k0 + kw, :], preferred_element_type=F32)
            k0 += kw
        o_ref[...] = acc

    @pl.when(i < n_prompt_tiles)
    def _():
        run(ap)

    @pl.when(i >= n_prompt_tiles)
    def _():
        run(asm)


def _mm_res_groups(parts, w, l, res, *, tn=1024):
    t, n = res.shape
    p_rows = parts[0][0].shape[0]
    s_rows = parts[0][1].shape[0]
    npt = p_rows // TM
    assert s_rows == TM and p_rows % TM == 0 and p_rows + s_rows == t
    kdim = w.shape[1]
    in_specs = ([pl.BlockSpec((TM, p[0].shape[1]), lambda i, j: (jnp.minimum(i, npt - 1), 0))
                 for p in parts]
                + [pl.BlockSpec((TM, p[1].shape[1]), lambda i, j: (0, 0)) for p in parts]
                + [pl.BlockSpec((None, kdim, tn), lambda i, j: (l, 0, j)),
                   pl.BlockSpec((TM, tn), lambda i, j: (i, j))])
    return pl.pallas_call(
        functools.partial(_mm_res_groups_kernel, n_parts=len(parts), n_prompt_tiles=npt),
        out_shape=jax.ShapeDtypeStruct((t, n), F32),
        grid=(t // TM, n // tn),
        in_specs=in_specs,
        out_specs=pl.BlockSpec((TM, tn), lambda i, j: (i, j)),
        compiler_params=_cparams("parallel", "parallel"),
        name="mm_res_groups",
    )(*[p[0] for p in parts], *[p[1] for p in parts], w, res)


def _pool_gmlp_kernel(xp_ref, u_ref, v_ref, hist_ref, pw_ref, ps_ref, vg_ref, ws_ref, bs_ref,
                      mix_ref, pst_ref, *rest, rows, pos0, want_v):
    if want_v:
        vrows_ref, full_ref, vpad_ref = rest
    else:
        full_ref, vpad_ref = rest
    r = pl.program_id(1)
    cw = xp_ref.shape[1]
    ng = len(POOL_WINDOWS)
    gd = cw // ng

    @pl.when(r == 0)
    def _():
        full_ref[0:POOL_PAD, :] = hist_ref[...]

    @pl.when(r > 0)
    def _():
        full_ref[0:POOL_PAD, :] = full_ref[rows:rows + POOL_PAD, :]

    x = xp_ref[...]
    full_ref[POOL_PAD:POOL_PAD + rows, :] = x
    pos = pos0 + r * rows + lax.broadcasted_iota(jnp.int32, (rows, gd), 0)
    for g, w in enumerate(POOL_WINDOWS):
        cs = slice(g * gd, (g + 1) * gd)
        acc = x[:, cs]
        for s in range(1, w):
            acc = acc + full_ref[POOL_PAD - s:POOL_PAD - s + rows, cs]
        cnt = jnp.minimum(pos + 1, w).astype(F32)
        dg = acc / cnt - x[:, cs]
        y = jnp.dot(dg.astype(BF16), pw_ref[g], preferred_element_type=F32)
        mix_ref[:, cs] = (y * ps_ref[:, cs]).astype(BF16)
    pst_ref[...] = full_ref[rows:rows + POOL_PAD, :]

    nh = ws_ref.shape[0]
    hd = v_ref.shape[1] // nh
    ck = ws_ref.shape[1]
    lc = min(rows, ck)
    tri = (lax.broadcasted_iota(jnp.int32, (lc, ck), 1)
           <= lax.broadcasted_iota(jnp.int32, (lc, ck), 0))
    if lc < ck:
        @pl.when(jnp.logical_and(pl.program_id(0) == 0, r == 0))
        def _():
            vpad_ref[...] = jnp.zeros_like(vpad_ref)
    for h in range(nh):
        hs = slice(h * hd, (h + 1) * hd)
        wsm = jnp.where(tri, ws_ref[h, 0:lc, :], 0.0).astype(BF16)
        bias = bs_ref[0:lc, hs]
        for c in range(rows // lc):
            rs = slice(c * lc, (c + 1) * lc)
            vh = _rms(v_ref[rs, hs], vg_ref[:, hs])
            if want_v:
                vrows_ref[rs, hs] = vh
            if lc < ck:
                vpad_ref[0:lc, :] = vh.astype(BF16)
                rhs = vpad_ref[...]
            else:
                rhs = vh.astype(BF16)
            s = jnp.dot(wsm, rhs, preferred_element_type=F32) + bias
            mix_ref[rs, cw + h * hd:cw + (h + 1) * hd] = (u_ref[rs, hs] * s).astype(BF16)


def _pool_gmlp(proj, hist, lh, pool_w, pool_scale, v_gain, ws, bs_full, l, *, row0, nseq, seqlen,
               rows, pos0, col0, want_v):
    cw = pool_w.shape[1] * pool_w.shape[2]
    gw = v_gain.shape[2]
    nt = seqlen // rows
    rb0 = row0 // rows
    cb = col0 // cw

    def rowmap(off):
        return lambda b, r: (rb0 + b * nt + r, cb + off)

    out_shape = [jax.ShapeDtypeStruct((nseq * seqlen, cw + gw), BF16),
                 jax.ShapeDtypeStruct((nseq, POOL_PAD, cw), F32)]
    out_specs = [pl.BlockSpec((rows, cw + gw), lambda b, r: (b * nt + r, 0)),
                 pl.BlockSpec((None, POOL_PAD, cw), lambda b, r: (b, 0, 0))]
    if want_v:
        out_shape.append(jax.ShapeDtypeStruct((nseq * seqlen, gw), F32))
        out_specs.append(pl.BlockSpec((rows, gw), lambda b, r: (b * nt + r, 0)))
    return pl.pallas_call(
        functools.partial(_pool_gmlp_kernel, rows=rows, pos0=pos0, want_v=want_v),
        out_shape=out_shape,
        grid=(nseq, nt),
        in_specs=[
            pl.BlockSpec((rows, cw), rowmap(0)),
            pl.BlockSpec((rows, gw), rowmap(1)),
            pl.BlockSpec((rows, gw), rowmap(2)),
            pl.BlockSpec((None, None, POOL_PAD, cw), lambda b, r: (lh, b, 0, 0)),
            _layer_spec(pool_w.shape[1:], l),
            _layer_spec(pool_scale.shape[1:], l),
            _layer_spec(v_gain.shape[1:], l),
            _layer_spec(ws.shape[1:], l),
            _layer_spec(bs_full.shape[1:], l),
        ],
        out_specs=out_specs,
        scratch_shapes=[pltpu.VMEM((rows + 2 * POOL_PAD, cw), F32),
                        pltpu.VMEM((ws.shape[2], gw // ws.shape[1]), BF16)],
        compiler_params=_cparams("arbitrary", "arbitrary"),
        name="pool_gmlp",
    )(proj, proj, proj, hist, pool_w, pool_scale, v_gain, ws, bs_full)


def _split3(x):
    hi = x.astype(BF16)
    r1 = x - hi.astype(F32)
    mid = r1.astype(BF16)
    lo = (r1 - mid.astype(F32)).astype(BF16)
    return hi, mid, lo


_NT = (((1,), (1,)), ((), ()))
_TN = (((0,), (0,)), ((), ()))


def _mlstm_kernel(q_ref, k_ref, v_ref, o_ref, gt_ref, gb_ref, ng_ref, c0_ref, n0_ref, m0_ref,
                  mix_ref, c_out, n_out, m_out, c_s, n_s, m_s, *pads, lq, nchunk):
    ck = LANE
    nh = c_s.shape[0]
    dh = c_s.shape[1]
    c = pl.program_id(1)

    @pl.when(c == 0)
    def _():
        c_s[...] = c0_ref[...]
        n_s[...] = n0_ref[...]
        m_s[...] = m0_ref[...]

    if lq < ck:
        kp_ref, vp_ref, gp_ref = pads

        @pl.when(jnp.logical_and(pl.program_id(0) == 0, c == 0))
        def _():
            kp_ref[...] = jnp.zeros_like(kp_ref)
            vp_ref[...] = jnp.zeros_like(vp_ref)
            gp_ref[...] = jnp.zeros_like(gp_ref)

        kp_ref[0:lq, :] = k_ref[...]
        vp_ref[0:lq, :] = v_ref[...]
        gp_ref[0:lq, :] = gt_ref[...]
        k_src, v_src, g_src = kp_ref, vp_ref, gp_ref
    else:
        k_src, v_src, g_src = k_ref, v_ref, gt_ref

    row = lax.broadcasted_iota(jnp.int32, (ck, ck), 0)
    col = lax.broadcasted_iota(jnp.int32, (ck, ck), 1)
    tri = col <= row
    valid = lax.broadcasted_iota(jnp.int32, (ck, 1), 0) < lq

    gts = g_src[...] + gb_ref[...]
    ig = gts[:, 0:LANE]
    lf = jax.nn.log_sigmoid(gts[:, LANE:2 * LANE])
    tri_b = jnp.where(tri, 1.0, 0.0).astype(BF16)
    b_all = sum(jnp.dot(tri_b, p, preferred_element_type=F32) for p in _split3(lf))
    r_all = ig - b_all
    m_all = m_s[...]
    inter_all = b_all + m_all
    bl_all = b_all[lq - 1:lq, :]
    wlog_all = bl_all + r_all
    m_new = jnp.maximum(bl_all + m_all,
                        jnp.max(jnp.where(valid, wlog_all, NEG_BIG), axis=0, keepdims=True))
    decay_all = jnp.exp(bl_all + m_all - m_new)
    wt_all = jnp.where(valid, jnp.exp(wlog_all - m_new), 0.0)

    sel_r = lax.broadcasted_iota(jnp.int32, (nh * ck, LANE), 0)
    sel_c = lax.broadcasted_iota(jnp.int32, (nh * ck, LANE), 1)
    sel = jnp.where(sel_c == lax.shift_right_logical(sel_r, ck.bit_length() - 1), 1.0, 0.0)
    sel = sel.astype(BF16)
    rrow = sum(lax.dot_general(sel, p, _NT, preferred_element_type=F32) for p in _split3(r_all))

    kscale = dh ** -0.5
    for h in range(nh):
        hs = slice(h * dh, (h + 1) * dh)
        qf = q_ref[:, hs]
        kf = k_src[:, hs] * kscale
        vf = v_src[:, hs]
        qb = qf.astype(BF16)
        kb = kf.astype(BF16)
        dm = b_all[0:lq, h:h + 1] + rrow[h * ck:h * ck + lq, :]
        dm = jnp.where(tri[0:lq, :], dm, NEG_BIG)
        inter = inter_all[0:lq, h:h + 1]
        m_row = jnp.maximum(inter, jnp.max(dm, axis=-1, keepdims=True))
        dexp = jnp.exp(dm - m_row)
        inter_w = jnp.exp(inter - m_row)
        s = lax.dot_general(qb, kb, _NT, preferred_element_type=F32) * dexp
        ch = c_s[h]
        nrow = n_s[h:h + 1, :]
        num = (jnp.dot(s.astype(BF16), vf.astype(BF16), preferred_element_type=F32)
               + inter_w * jnp.dot(qb, ch.astype(BF16), preferred_element_type=F32))
        den = (jnp.sum(s, axis=-1, keepdims=True)
               + inter_w * jnp.sum(qf * nrow, axis=-1, keepdims=True))
        hv = num / jnp.maximum(jnp.abs(den), jnp.exp(-m_row))
        hn = _rms(hv, ng_ref[:, hs])
        mix_ref[:, hs] = (hn * jax.nn.sigmoid(o_ref[:, hs])).astype(BF16)

        wt = wt_all[:, h:h + 1]
        decay = decay_all[:, h:h + 1]
        wv = (wt * vf).astype(BF16)
        c_s[h] = decay * ch + lax.dot_general(kb, wv, _TN, preferred_element_type=F32)
        n_s[h:h + 1, :] = decay * nrow + jnp.sum(wt * kf, axis=0, keepdims=True)

    m_s[...] = m_new

    @pl.when(c == nchunk - 1)
    def _():
        c_out[...] = c_s[...]
        n_out[...] = n_s[...]
        m_out[...] = m_s[...]


def _mlstm(proj, gates, gbias, ngain, l, c0, n0, m0, ls, *, row0, nseq, seqlen):
    nh, dh = c0.shape[2], c0.shape[3]
    w = nh * dh
    lq = min(seqlen, LANE)
    nchunk = seqlen // lq
    rb0 = row0 // lq

    def rowmap(off):
        return lambda b, c: (rb0 + b * nchunk + c, off)

    pads = []
    if lq < LANE:
        pads = [pltpu.VMEM((LANE, w), F32), pltpu.VMEM((LANE, w), F32),
                pltpu.VMEM((LANE, 2 * LANE), F32)]
    return pl.pallas_call(
        functools.partial(_mlstm_kernel, lq=lq, nchunk=nchunk),
        out_shape=[jax.ShapeDtypeStruct((nseq * seqlen, w), BF16),
                   jax.ShapeDtypeStruct((nseq, nh, dh, dh), F32),
                   jax.ShapeDtypeStruct((nseq, nh, dh), F32),
                   jax.ShapeDtypeStruct((nseq, 1, LANE), F32)],
        grid=(nseq, nchunk),
        in_specs=[
            pl.BlockSpec((lq, w), rowmap(0)),
            pl.BlockSpec((lq, w), rowmap(1)),
            pl.BlockSpec((lq, w), rowmap(2)),
            pl.BlockSpec((lq, w), rowmap(3)),
            pl.BlockSpec((lq, 2 * LANE), rowmap(0)),
            _layer_spec((1, 2 * LANE), l),
            _layer_spec((1, w), l),
            pl.BlockSpec((None, None, nh, dh, dh), lambda b, c: (ls, b, 0, 0, 0)),
            pl.BlockSpec((None, None, nh, dh), lambda b, c: (ls, b, 0, 0)),
            pl.BlockSpec((None, None, 1, LANE), lambda b, c: (ls, b, 0, 0)),
        ],
        out_specs=[
            pl.BlockSpec((lq, w), lambda b, c: (b * nchunk + c, 0)),
            pl.BlockSpec((None, nh, dh, dh), lambda b, c: (b, 0, 0, 0)),
            pl.BlockSpec((None, nh, dh), lambda b, c: (b, 0, 0)),
            pl.BlockSpec((None, 1, LANE), lambda b, c: (b, 0, 0)),
        ],
        scratch_shapes=[pltpu.VMEM((nh, dh, dh), F32), pltpu.VMEM((nh, dh), F32),
                        pltpu.VMEM((1, LANE), F32)] + pads,
        compiler_params=_cparams("arbitrary", "arbitrary"),
        name="mlstm",
    )(proj, proj, proj, proj, gates, gbias, ngain, c0, n0, m0)


def _attn_kernel(q_ref, k_ref, v_ref, o_ref):
    d = q_ref.shape[1]
    hd = d // MEM_HEADS
    scale = hd ** -0.5
    for h in range(MEM_HEADS):
        hs = slice(h * hd, (h + 1) * hd)
        kb = k_ref[:, hs].astype(BF16)
        vb = v_ref[:, hs].astype(BF16)
        s = lax.dot_general(q_ref[:, hs], kb, _NT, preferred_element_type=F32) * scale
        p = jnp.exp(s - jnp.max(s, axis=-1, keepdims=True))
        p = p / jnp.sum(p, axis=-1, keepdims=True)
        o_ref[:, hs] = jnp.dot(p.astype(BF16), vb, preferred_element_type=F32).astype(BF16)


def _attn(q, mem_k, mem_v, l, *, row0, nseq, seqlen, rows):
    d = q.shape[1]
    nm = mem_k.shape[2]
    nt = seqlen // rows
    rb0 = row0 // rows
    kv_spec = pl.BlockSpec((None, None, nm, d), lambda b, r: (l, b, 0, 0))
    return pl.pallas_call(
        _attn_kernel,
        out_shape=jax.ShapeDtypeStruct((nseq * seqlen, d), BF16),
        grid=(nseq, nt),
        in_specs=[pl.BlockSpec((rows, d), lambda b, r: (rb0 + b * nt + r, 0)), kv_spec, kv_spec],
        out_specs=pl.BlockSpec((rows, d), lambda b, r: (b * nt + r, 0)),
        compiler_params=_cparams("parallel", "arbitrary"),
        name="mem_attn",
    )(q, mem_k, mem_v)


def _final_norm_kernel(x_ref, g_ref, o_ref):
    o_ref[...] = _rms(x_ref[...], g_ref[...])


def _final_norm(x, g, *, row0, nrows, rows=512):
    d = x.shape[1]
    rb0 = row0 // rows
    return pl.pallas_call(
        _final_norm_kernel,
        out_shape=jax.ShapeDtypeStruct((nrows, d), F32),
        grid=(nrows // rows,),
        in_specs=[pl.BlockSpec((rows, d), lambda i: (rb0 + i, 0)),
                  pl.BlockSpec((1, d), lambda i: (0, 0))],
        out_specs=pl.BlockSpec((rows, d), lambda i: (i, 0)),
        compiler_params=_cparams("parallel"),
        name="final_norm",
    )(x, g)


def kernel(x_prompt, x_sample, mem_prompt, state_pool, state_mlstm_C, state_mlstm_n, state_mlstm_m, cache_mem_k, cache_mem_v, g_ffn1, w_ffn1_in, w_ffn1_out, g_mix, w_in, pool_w, pool_scale, gmlp_v_gain, gmlp_ws, gmlp_bs, mlstm_i_bias, mlstm_f_bias, mlstm_norm_gain, w_out, g_xattn, g_mem, w_mem_q, w_mem_k, w_mem_v, w_mem_o, g_ffn2, w_ffn2_in, w_ffn2_out, g_final):
    nb, seq, d = x_prompt.shape
    db, dseq, _ = x_sample.shape
    depth = g_ffn1.shape[0]
    nh, dh = state_mlstm_C.shape[2], state_mlstm_C.shape[3]
    mw = nh * dh
    pw = pool_w.shape[1] * pool_w.shape[2]
    gh = gmlp_ws.shape[1]
    gw = gmlp_v_gain.shape[1]
    ghd = gw // gh
    nmem = mem_prompt.shape[1]
    npool = state_pool.shape[2]
    p_rows = nb * seq
    s_rows = db * dseq
    past = seq
    main_cols = pw + 2 * gw + 4 * mw
    col_tile = 512
    n_col_tiles = main_cols // col_tile
    mix_tiles = (pw + 2 * gw) // col_tile

    x = jnp.concatenate([x_prompt.reshape(p_rows, d), x_sample.reshape(s_rows, d)], axis=0)
    rows3 = lambda v: v.reshape(depth, 1, -1)

    w1i, w1o = w_ffn1_in.astype(BF16), w_ffn1_out.astype(BF16)
    w2i, w2o = w_ffn2_in.astype(BF16), w_ffn2_out.astype(BF16)
    w_in_b = w_in.astype(BF16)
    zpad = jnp.zeros((depth, d, LANE - nh), F32)
    w_gate = jnp.concatenate([w_in[:, :, main_cols:main_cols + nh], zpad,
                              w_in[:, :, main_cols + nh:], zpad], axis=2).astype(BF16)
    bpad = jnp.zeros((depth, LANE - nh), F32)
    gbias = rows3(jnp.concatenate([mlstm_i_bias, bpad, mlstm_f_bias, bpad], axis=1))
    w_out_b, wq_b, wo_b = w_out.astype(BF16), w_mem_q.astype(BF16), w_mem_o.astype(BF16)
    pool_w_b = pool_w.astype(BF16)
    bs_full = jnp.repeat(jnp.swapaxes(gmlp_bs, 1, 2), ghd, axis=2)
    hist_p = jnp.zeros((1, nb, POOL_PAD, pw), F32)
    hist_s = jnp.pad(state_pool, ((0, 0), (0, 0), (POOL_PAD - npool, 0), (0, 0)))
    c0_p = jnp.zeros((1, nb, nh, dh, dh), F32)
    n0_p = jnp.zeros((1, nb, nh, dh), F32)
    m0_p = jnp.zeros((1, nb, 1, LANE), F32)
    c0_s = state_mlstm_C.astype(F32)
    n0_s = state_mlstm_n.astype(F32)
    m0_s = jnp.pad(state_mlstm_m.astype(F32), ((0, 0), (0, 0), (0, LANE - nh)))[:, :, None, :]

    mk, mv = _mem_kv(mem_prompt.reshape(nb * nmem, d), rows3(g_mem), w_mem_k.astype(BF16),
                     w_mem_v.astype(BF16))
    mk = mk.reshape(depth, nb, nmem, d)
    mv = mv.reshape(depth, nb, nmem, d)

    outs = {k: [] for k in ("pool_p", "pool_s", "c_p", "n_p", "m_p", "c_s", "n_s", "m_s", "v_s")}
    for l in range(depth):
        act = _ffn_in(x, rows3(g_ffn1), w1i, l)
        x = _mm_res(act, w1o, l, x, scale=0.5)

        proj, gates = _norm_mm(x, rows3(g_mix), w_in_b, l, tn=col_tile, n_tiles=n_col_tiles,
                               out_dtype=F32, tile_shift=n_col_tiles - mix_tiles, w_extra=w_gate)

        common = (pool_w_b, rows3(pool_scale), rows3(gmlp_v_gain), gmlp_ws, bs_full, l)
        mix_a_p, pst_p = _pool_gmlp(proj, hist_p, 0, *common, row0=0, nseq=nb, seqlen=seq,
                                    rows=512, pos0=0, col0=4 * mw, want_v=False)
        mix_a_s, pst_s, v_rows = _pool_gmlp(proj, hist_s, l, *common, row0=p_rows, nseq=db,
                                            seqlen=dseq, rows=dseq, pos0=past, col0=4 * mw,
                                            want_v=True)
        outs["pool_p"].append(pst_p[:, POOL_PAD - npool:])
        outs["pool_s"].append(pst_s[:, POOL_PAD - npool:])
        outs["v_s"].append(v_rows.reshape(db, dseq, gw))

        ngain = rows3(mlstm_norm_gain)
        mix_b_p, c_p, n_p, m_p = _mlstm(proj, gates, gbias, ngain, l, c0_p, n0_p, m0_p, 0,
                                        row0=0, nseq=nb, seqlen=seq)
        mix_b_s, c_s, n_s, m_s = _mlstm(proj, gates, gbias, ngain, l, c0_s, n0_s, m0_s, l,
                                        row0=p_rows, nseq=db, seqlen=dseq)
        outs["c_p"].append(c_p)
        outs["n_p"].append(n_p)
        outs["m_p"].append(m_p[:, 0, :nh])
        outs["c_s"].append(c_s)
        outs["n_s"].append(n_s)
        outs["m_s"].append(m_s[:, 0, :nh])

        x = _mm_res_groups([(mix_a_p, mix_a_s), (mix_b_p, mix_b_s)], w_out_b, l, x)

        q = _norm_mm(x, rows3(g_xattn), wq_b, l, tn=1024, n_tiles=d // 1024, out_dtype=BF16)
        att_p = _attn(q, mk, mv, l, row0=0, nseq=nb, seqlen=seq, rows=1024)
        att_s = _attn(q, cache_mem_k, cache_mem_v, l, row0=p_rows, nseq=db, seqlen=dseq,
                      rows=dseq)
        x = _mm_res_groups([(att_p, att_s)], wo_b, l, x)

        act = _ffn_in(x, rows3(g_ffn2), w2i, l)
        x = _mm_res(act, w2o, l, x, scale=0.5)

    gfin = g_final.reshape(1, d)
    y_prompt = _final_norm(x, gfin, row0=0, nrows=p_rows).reshape(nb, seq, d)
    y_sample = _final_norm(x, gfin, row0=p_rows, nrows=s_rows).reshape(db, dseq, d)
    st = lambda k: jnp.stack(outs[k])
    return (y_prompt, y_sample, st("pool_p"), st("pool_s"), st("c_p"), st("n_p"), st("m_p"),
            st("c_s"), st("n_s"), st("m_s"), st("v_s"), mk, mv)
```

```python
import functools

import jax
import jax.numpy as jnp
from jax import lax
from jax.experimental import pallas as pl
from jax.experimental.pallas import tpu as pltpu

F32 = jnp.float32
BF16 = jnp.bfloat16

EPS = 1e-6
POOL_WINDOWS = (2, 4, 8, 16)
POOL_PAD = 16
MEM_HEADS = 4
LANE = 128
VMEM_LIMIT = 56 * 1024 * 1024
NEG_BIG = -1e30

TM = 1024
TB = 2304
TBM = 1536
TR = 512
MM_ROWS = 1024
NORM_ROWS = 256


def _cparams(*sem):
    return pltpu.CompilerParams(dimension_semantics=sem, vmem_limit_bytes=VMEM_LIMIT)


def _rms(x, g):
    ms = jnp.mean(x * x, axis=-1, keepdims=True)
    return x * lax.rsqrt(ms + EPS) * g


def _layer_spec(shape, l):
    zeros = (0,) * len(shape)
    return pl.BlockSpec((None,) + tuple(shape), lambda *_: (l,) + zeros)


def _emit_norm_inputs(x, g_ref, xg_ref, ssq_ref):
    xg_ref[...] = (x * g_ref[...]).astype(BF16)
    ssq_ref[...] = jnp.broadcast_to(jnp.sum(x * x, axis=-1, keepdims=True), ssq_ref.shape)


def _rstd(ssq_ref, d, width):
    s = ssq_ref[0]
    for p in range(1, ssq_ref.shape[0]):
        s = s + ssq_ref[p]
    r = lax.rsqrt(s / d + EPS)
    reps = width // LANE
    return r if reps == 1 else jnp.concatenate([r] * reps, axis=1)


def _ssq_spec(ssq, rows):
    return pl.BlockSpec((ssq.shape[0], rows, LANE), lambda i, *_: (0, i, 0))


def _prep_kernel(xp_ref, xs_ref, g_ref, x_ref, xg_ref, ssq_ref, *, n_prompt_tiles):
    i = pl.program_id(0)

    def emit(src):
        x = src[...]
        x_ref[...] = x
        _emit_norm_inputs(x, g_ref, xg_ref, ssq_ref)

    @pl.when(i < n_prompt_tiles)
    def _():
        emit(xp_ref)

    @pl.when(i >= n_prompt_tiles)
    def _():
        emit(xs_ref)


def _prep(xp, xs, g, l):
    p_rows, d = xp.shape
    t = p_rows + xs.shape[0]
    npt = p_rows // TR
    return pl.pallas_call(
        functools.partial(_prep_kernel, n_prompt_tiles=npt),
        out_shape=[jax.ShapeDtypeStruct((t, d), F32), jax.ShapeDtypeStruct((t, d), BF16),
                   jax.ShapeDtypeStruct((1, t, LANE), F32)],
        grid=(t // TR,),
        in_specs=[pl.BlockSpec((TR, d), lambda i: (jnp.minimum(i, npt - 1), 0)),
                  pl.BlockSpec((TR, d), lambda i: (jnp.maximum(i - npt, 0), 0)),
                  _layer_spec((1, d), l)],
        out_specs=[pl.BlockSpec((TR, d), lambda i: (i, 0)),
                   pl.BlockSpec((TR, d), lambda i: (i, 0)),
                   pl.BlockSpec((None, TR, LANE), lambda i: (0, i, 0))],
        compiler_params=_cparams("arbitrary"),
        name="prep",
    )(xp, xs, g)


def _row_chunks(rows):
    n = -(-rows // MM_ROWS)
    assert rows % n == 0
    return [slice(c * (rows // n), (c + 1) * (rows // n)) for c in range(n)]


def _ffn_in_kernel(xg_ref, ssq_ref, wg_ref, wu_ref, wo_ref, act_ref, wob_ref):
    wg = wg_ref[...].astype(BF16)
    wu = wu_ref[...].astype(BF16)
    for rows in _row_chunks(xg_ref.shape[0]):
        rs = _rstd(ssq_ref.at[:, rows, :], xg_ref.shape[1], act_ref.shape[1])
        xg = xg_ref[rows, :]
        gate = jnp.dot(xg, wg, preferred_element_type=F32) * rs
        up = jnp.dot(xg, wu, preferred_element_type=F32) * rs
        act_ref[rows, :] = (gate * jax.nn.sigmoid(gate) * up).astype(BF16)
    wob_ref[...] = wo_ref[...].astype(BF16)


def _ffn_in(xg, ssq, w, w_down, l, *, tf=256):
    t, d = xg.shape
    f = w.shape[2] // 2
    nf = f // tf
    nm = t // TB
    rb = f // (nm * nf)
    assert rb * nm * nf == f and rb % 16 == 0
    return pl.pallas_call(
        _ffn_in_kernel,
        out_shape=[jax.ShapeDtypeStruct((t, f), BF16), jax.ShapeDtypeStruct((f, d), BF16)],
        grid=(nm, nf),
        in_specs=[
            pl.BlockSpec((TB, d), lambda i, j: (i, 0)),
            _ssq_spec(ssq, TB),
            pl.BlockSpec((None, d, tf), lambda i, j: (l, 0, j)),
            pl.BlockSpec((None, d, tf), lambda i, j: (l, 0, j + nf)),
            pl.BlockSpec((None, rb, d), lambda i, j: (l, i * nf + j, 0)),
        ],
        out_specs=[pl.BlockSpec((TB, tf), lambda i, j: (i, j)),
                   pl.BlockSpec((rb, d), lambda i, j: (i * nf + j, 0))],
        compiler_params=_cparams("arbitrary", "arbitrary"),
        name="ffn_in",
    )(xg, ssq, w, w, w_down)


def _mix_in_kernel(xg_ref, ssq_ref, w_ref, we_ref, main_ref, mixf_ref, gates_ref, *, n_f32_tiles):
    j = pl.program_id(1)
    d = xg_ref.shape[1]

    def project(w_ref, o_ref):
        w = w_ref[...].astype(BF16)
        for rows in _row_chunks(xg_ref.shape[0]):
            y = jnp.dot(xg_ref[rows, :], w, preferred_element_type=F32)
            o_ref[rows, :] = (y * _rstd(ssq_ref.at[:, rows, :], d, w.shape[1])).astype(o_ref.dtype)

    @pl.when(j < n_f32_tiles)
    def _():
        project(w_ref, mixf_ref)

    @pl.when(j >= n_f32_tiles)
    def _():
        project(w_ref, main_ref)

    @pl.when(j == 0)
    def _():
        project(we_ref, gates_ref)


def _mix_in(xg, ssq, w, w_gate, l, *, n_f32_cols, n_cols, tn=512):
    t, d = xg.shape
    nf32 = n_f32_cols // tn
    nt = n_cols // tn
    ne = w_gate.shape[2]
    return pl.pallas_call(
        functools.partial(_mix_in_kernel, n_f32_tiles=nf32),
        out_shape=[jax.ShapeDtypeStruct((t, n_cols - n_f32_cols), BF16),
                   jax.ShapeDtypeStruct((t, n_f32_cols), F32),
                   jax.ShapeDtypeStruct((t, ne), F32)],
        grid=(t // TBM, nt),
        in_specs=[
            pl.BlockSpec((TBM, d), lambda i, j: (i, 0)),
            _ssq_spec(ssq, TBM),
            pl.BlockSpec((None, d, tn), lambda i, j: (l, 0, j)),
            _layer_spec((d, ne), l),
        ],
        out_specs=[pl.BlockSpec((TBM, tn), lambda i, j: (i, jnp.maximum(j - nf32, 0))),
                   pl.BlockSpec((TBM, tn), lambda i, j: (i, jnp.minimum(j, nf32 - 1))),
                   pl.BlockSpec((TBM, ne), lambda i, j: (i, 0))],
        compiler_params=_cparams("arbitrary", "arbitrary"),
        name="mix_in",
    )(xg, ssq, w, w_gate)


def _scaled_mm_kernel(xg_ref, ssq_ref, w_ref, o_ref):
    w = w_ref[...].astype(BF16)
    for rows in _row_chunks(xg_ref.shape[0]):
        y = jnp.dot(xg_ref[rows, :], w, preferred_element_type=F32)
        rs = _rstd(ssq_ref.at[:, rows, :], xg_ref.shape[1], o_ref.shape[1])
        o_ref[rows, :] = (y * rs).astype(o_ref.dtype)


def _scaled_mm(xg, ssq, w, l, *, tn=512):
    t, d = xg.shape
    n = w.shape[2]
    return pl.pallas_call(
        _scaled_mm_kernel,
        out_shape=jax.ShapeDtypeStruct((t, n), BF16),
        grid=(t // TB, n // tn),
        in_specs=[pl.BlockSpec((TB, d), lambda i, j: (i, 0)),
                  _ssq_spec(ssq, TB),
                  pl.BlockSpec((None, d, tn), lambda i, j: (l, 0, j))],
        out_specs=pl.BlockSpec((TB, tn), lambda i, j: (i, j)),
        compiler_params=_cparams("parallel", "arbitrary"),
        name="scaled_mm",
    )(xg, ssq, w)


def _norm_rows_to(x_ref, g_ref, h_ref):
    rows = x_ref.shape[0]

    def body(r, carry):
        sl = pl.ds(pl.multiple_of(r * NORM_ROWS, NORM_ROWS), NORM_ROWS)
        h_ref[sl, :] = _rms(x_ref[sl, :], g_ref[...]).astype(BF16)
        return carry

    lax.fori_loop(0, rows // NORM_ROWS, body, 0)


def _mem_kv_kernel(x_ref, g_ref, wk_ref, wv_ref, ok_ref, ov_ref, h_ref):
    @pl.when(pl.program_id(1) == 0)
    def _():
        _norm_rows_to(x_ref, g_ref, h_ref)

    h = h_ref[...]
    ok_ref[...] = jnp.dot(h, wk_ref[...].astype(BF16), preferred_element_type=F32)
    ov_ref[...] = jnp.dot(h, wv_ref[...].astype(BF16), preferred_element_type=F32)


def _mem_kv(mem, g, wk, wv, *, tn=512):
    rows, d = mem.shape
    nl = g.shape[0]
    out = jax.ShapeDtypeStruct((nl, rows, d), F32)
    w_spec = pl.BlockSpec((None, d, tn), lambda ll, j: (ll, 0, j))
    o_spec = pl.BlockSpec((None, rows, tn), lambda ll, j: (ll, 0, j))
    return pl.pallas_call(
        _mem_kv_kernel,
        out_shape=[out, out],
        grid=(nl, d // tn),
        in_specs=[pl.BlockSpec((rows, d), lambda ll, j: (0, 0)),
                  pl.BlockSpec((None, 1, d), lambda ll, j: (ll, 0, 0)),
                  w_spec, w_spec],
        out_specs=[o_spec, o_spec],
        scratch_shapes=[pltpu.VMEM((rows, d), BF16)],
        compiler_params=_cparams("arbitrary", "arbitrary"),
        name="mem_kv",
    )(mem, g, wk, wv)


def _mm_res_kernel(a_ref, w_ref, r_ref, *rest, scale, nk, emit):
    if emit:
        g_ref, o_ref, xg_ref, ssq_ref = rest
    else:
        (o_ref,) = rest
    k = pl.program_id(2)

    def part():
        return scale * jnp.dot(a_ref[...], w_ref[...], preferred_element_type=F32)

    @pl.when(k == 0)
    def _():
        o_ref[...] = r_ref[...] + part()

    @pl.when(jnp.logical_and(k > 0, k < nk - 1))
    def _():
        o_ref[...] += part()

    @pl.when(k == nk - 1)
    def _():
        x = o_ref[...] + part()
        o_ref[...] = x
        if emit:
            _emit_norm_inputs(x, g_ref, xg_ref, ssq_ref)


def _mm_res(a, w, res, *, scale, g_next=None, tn=1024, tk=2048):
    t, kdim = a.shape
    n = w.shape[1]
    nk = kdim // tk
    assert nk >= 2
    emit = g_next is not None
    in_specs = [pl.BlockSpec((TM, tk), lambda i, j, k: (i, k)),
                pl.BlockSpec((tk, tn), lambda i, j, k: (k, j)),
                pl.BlockSpec((TM, tn), lambda i, j, k: (i, j))]
    out_shape = [jax.ShapeDtypeStruct((t, n), F32)]
    out_specs = [pl.BlockSpec((TM, tn), lambda i, j, k: (i, j))]
    args = [a, w, res]
    if emit:
        gains, ln = g_next
        in_specs.append(pl.BlockSpec((None, 1, tn), lambda i, j, k: (ln, 0, j)))
        args.append(gains)
        out_shape += [jax.ShapeDtypeStruct((t, n), BF16),
                      jax.ShapeDtypeStruct((n // tn, t, LANE), F32)]
        out_specs += [pl.BlockSpec((TM, tn), lambda i, j, k: (i, j)),
                      pl.BlockSpec((None, TM, LANE), lambda i, j, k: (j, i, 0))]
    res_out = pl.pallas_call(
        functools.partial(_mm_res_kernel, scale=scale, nk=nk, emit=emit),
        out_shape=out_shape,
        grid=(t // TM, n // tn, nk),
        in_specs=in_specs,
        out_specs=out_specs,
        compiler_params=_cparams("parallel", "parallel", "arbitrary"),
        name="mm_res",
    )(*args)
    return res_out if emit else res_out[0]


def _mm_res_groups_kernel(*refs, n_parts, n_prompt_tiles):
    ap = refs[0:n_parts]
    asm = refs[n_parts:2 * n_parts]
    w_ref, r_ref, g_ref, o_ref, xg_ref, ssq_ref = refs[2 * n_parts:]
    i = pl.program_id(0)

    def run(parts):
        acc = r_ref[...]
        k0 = 0
        for p in parts:
            kw = p.shape[1]
            acc = acc + jnp.dot(p[...], w_ref[k0:k0 + kw, :], preferred_element_type=F32)
            k0 += kw
        o_ref[...] = acc
        _emit_norm_inputs(acc, g_ref, xg_ref, ssq_ref)

    @pl.when(i < n_prompt_tiles)
    def _():
        run(ap)

    @pl.when(i >= n_prompt_tiles)
    def _():
        run(asm)


def _mm_res_groups(parts, w, l, res, g_next):
    t, n = res.shape
    p_rows = parts[0][0].shape[0]
    npt = p_rows // TR
    assert p_rows % TR == 0 and (t - p_rows) % TR == 0
    kdim = w.shape[1]
    gains, ln = g_next
    in_specs = ([pl.BlockSpec((TR, p[0].shape[1]), lambda i: (jnp.minimum(i, npt - 1), 0))
                 for p in parts]
                + [pl.BlockSpec((TR, p[1].shape[1]), lambda i: (jnp.maximum(i - npt, 0), 0))
                   for p in parts]
                + [_layer_spec((kdim, n), l),
                   pl.BlockSpec((TR, n), lambda i: (i, 0)),
                   _layer_spec((1, n), ln)])
    return pl.pallas_call(
        functools.partial(_mm_res_groups_kernel, n_parts=len(parts), n_prompt_tiles=npt),
        out_shape=[jax.ShapeDtypeStruct((t, n), F32), jax.ShapeDtypeStruct((t, n), BF16),
                   jax.ShapeDtypeStruct((1, t, LANE), F32)],
        grid=(t // TR,),
        in_specs=in_specs,
        out_specs=[pl.BlockSpec((TR, n), lambda i: (i, 0)),
                   pl.BlockSpec((TR, n), lambda i: (i, 0)),
                   pl.BlockSpec((None, TR, LANE), lambda i: (0, i, 0))],
        compiler_params=_cparams("arbitrary"),
        name="mm_res_groups",
    )(*[p[0] for p in parts], *[p[1] for p in parts], w, res, gains)


def _pool_gmlp_kernel(xp_ref, u_ref, v_ref, hist_ref, pw_ref, ps_ref, vg_ref, ws_ref, bs_ref,
                      mix_ref, pst_ref, *rest, rows, pos0, want_v):
    if want_v:
        vrows_ref, full_ref, vpad_ref = rest
    else:
        full_ref, vpad_ref = rest
    r = pl.program_id(1)
    cw = xp_ref.shape[1]
    ng = len(POOL_WINDOWS)
    gd = cw // ng

    @pl.when(r == 0)
    def _():
        full_ref[0:POOL_PAD, :] = hist_ref[...]

    @pl.when(r > 0)
    def _():
        full_ref[0:POOL_PAD, :] = full_ref[rows:rows + POOL_PAD, :]

    x = xp_ref[...]
    full_ref[POOL_PAD:POOL_PAD + rows, :] = x
    pos = pos0 + r * rows + lax.broadcasted_iota(jnp.int32, (rows, gd), 0)
    for g, w in enumerate(POOL_WINDOWS):
        cs = slice(g * gd, (g + 1) * gd)
        acc = x[:, cs]
        for s in range(1, w):
            acc = acc + full_ref[POOL_PAD - s:POOL_PAD - s + rows, cs]
        cnt = jnp.minimum(pos + 1, w).astype(F32)
        dg = acc / cnt - x[:, cs]
        y = jnp.dot(dg.astype(BF16), pw_ref[g], preferred_element_type=F32)
        mix_ref[:, cs] = (y * ps_ref[:, cs]).astype(BF16)
    pst_ref[...] = full_ref[rows:rows + POOL_PAD, :]

    nh = ws_ref.shape[0]
    hd = v_ref.shape[1] // nh
    ck = ws_ref.shape[1]
    lc = min(rows, ck)
    tri = (lax.broadcasted_iota(jnp.int32, (lc, ck), 1)
           <= lax.broadcasted_iota(jnp.int32, (lc, ck), 0))
    if lc < ck:
        @pl.when(jnp.logical_and(pl.program_id(0) == 0, r == 0))
        def _():
            vpad_ref[...] = jnp.zeros_like(vpad_ref)
    for h in range(nh):
        hs = slice(h * hd, (h + 1) * hd)
        wsm = jnp.where(tri, ws_ref[h, 0:lc, :], 0.0).astype(BF16)
        bias = bs_ref[0:lc, hs]
        for c in range(rows // lc):
            rs = slice(c * lc, (c + 1) * lc)
            vh = _rms(v_ref[rs, hs], vg_ref[:, hs])
            if want_v:
                vrows_ref[rs, hs] = vh
            if lc < ck:
                vpad_ref[0:lc, :] = vh.astype(BF16)
                rhs = vpad_ref[...]
            else:
                rhs = vh.astype(BF16)
            s = jnp.dot(wsm, rhs, preferred_element_type=F32) + bias
            mix_ref[rs, cw + h * hd:cw + (h + 1) * hd] = (u_ref[rs, hs] * s).astype(BF16)


def _pool_gmlp(proj, hist, lh, pool_w, pool_scale, v_gain, ws, bs_full, l, *, row0, nseq, seqlen,
               rows, pos0, want_v):
    cw = pool_w.shape[1] * pool_w.shape[2]
    gw = v_gain.shape[2]
    nt = seqlen // rows
    rb0 = row0 // rows

    def rowmap(off):
        return lambda b, r: (rb0 + b * nt + r, off)

    out_shape = [jax.ShapeDtypeStruct((nseq * seqlen, cw + gw), BF16),
                 jax.ShapeDtypeStruct((nseq, POOL_PAD, cw), F32)]
    out_specs = [pl.BlockSpec((rows, cw + gw), lambda b, r: (b * nt + r, 0)),
                 pl.BlockSpec((None, POOL_PAD, cw), lambda b, r: (b, 0, 0))]
    if want_v:
        out_shape.append(jax.ShapeDtypeStruct((nseq * seqlen, gw), F32))
        out_specs.append(pl.BlockSpec((rows, gw), lambda b, r: (b * nt + r, 0)))
    return pl.pallas_call(
        functools.partial(_pool_gmlp_kernel, rows=rows, pos0=pos0, want_v=want_v),
        out_shape=out_shape,
        grid=(nseq, nt),
        in_specs=[
            pl.BlockSpec((rows, cw), rowmap(0)),
            pl.BlockSpec((rows, gw), rowmap(1)),
            pl.BlockSpec((rows, gw), rowmap(2)),
            pl.BlockSpec((None, None, POOL_PAD, cw), lambda b, r: (lh, b, 0, 0)),
            _layer_spec(pool_w.shape[1:], l),
            _layer_spec(pool_scale.shape[1:], l),
            _layer_spec(v_gain.shape[1:], l),
            _layer_spec(ws.shape[1:], l),
            _layer_spec(bs_full.shape[1:], l),
        ],
        out_specs=out_specs,
        scratch_shapes=[pltpu.VMEM((rows + 2 * POOL_PAD, cw), F32),
                        pltpu.VMEM((ws.shape[2], gw // ws.shape[1]), BF16)],
        compiler_params=_cparams("arbitrary", "arbitrary"),
        name="pool_gmlp",
    )(proj, proj, proj, hist, pool_w, pool_scale, v_gain, ws, bs_full)


def _split3(x):
    hi = x.astype(BF16)
    r1 = x - hi.astype(F32)
    mid = r1.astype(BF16)
    lo = (r1 - mid.astype(F32)).astype(BF16)
    return hi, mid, lo


_NT = (((1,), (1,)), ((), ()))
_TN = (((0,), (0,)), ((), ()))


def _mlstm_kernel(q_ref, k_ref, v_ref, o_ref, gt_ref, gb_ref, ng_ref, c0_ref, n0_ref, m0_ref,
                  mix_ref, c_out, n_out, m_out, c_s, n_s, m_s, *pads, lq, nchunk):
    ck = LANE
    nh = c_s.shape[0]
    dh = c_s.shape[1]
    c = pl.program_id(1)

    @pl.when(c == 0)
    def _():
        c_s[...] = c0_ref[...]
        n_s[...] = n0_ref[...]
        m_s[...] = m0_ref[...]

    if lq < ck:
        kp_ref, vp_ref, gp_ref = pads

        @pl.when(jnp.logical_and(pl.program_id(0) == 0, c == 0))
        def _():
            kp_ref[...] = jnp.zeros_like(kp_ref)
            vp_ref[...] = jnp.zeros_like(vp_ref)
            gp_ref[...] = jnp.zeros_like(gp_ref)

        kp_ref[0:lq, :] = k_ref[...]
        vp_ref[0:lq, :] = v_ref[...]
        gp_ref[0:lq, :] = gt_ref[...]
        k_src, v_src, g_src = kp_ref, vp_ref, gp_ref
    else:
        k_src, v_src, g_src = k_ref, v_ref, gt_ref

    row = lax.broadcasted_iota(jnp.int32, (ck, ck), 0)
    col = lax.broadcasted_iota(jnp.int32, (ck, ck), 1)
    tri = col <= row
    valid = lax.broadcasted_iota(jnp.int32, (ck, 1), 0) < lq

    gts = g_src[...] + gb_ref[...]
    ig = gts[:, 0:LANE]
    lf = jax.nn.log_sigmoid(gts[:, LANE:2 * LANE])
    tri_b = jnp.where(tri, 1.0, 0.0).astype(BF16)
    b_all = sum(jnp.dot(tri_b, p, preferred_element_type=F32) for p in _split3(lf))
    r_all = ig - b_all
    m_all = m_s[...]
    inter_all = b_all + m_all
    bl_all = b_all[lq - 1:lq, :]
    wlog_all = bl_all + r_all
    m_new = jnp.maximum(bl_all + m_all,
                        jnp.max(jnp.where(valid, wlog_all, NEG_BIG), axis=0, keepdims=True))
    decay_all = jnp.exp(bl_all + m_all - m_new)
    wt_all = jnp.where(valid, jnp.exp(wlog_all - m_new), 0.0)

    sel_r = lax.broadcasted_iota(jnp.int32, (nh * ck, LANE), 0)
    sel_c = lax.broadcasted_iota(jnp.int32, (nh * ck, LANE), 1)
    sel = jnp.where(sel_c == lax.shift_right_logical(sel_r, ck.bit_length() - 1), 1.0, 0.0)
    sel = sel.astype(BF16)
    rrow = sum(lax.dot_general(sel, p, _NT, preferred_element_type=F32) for p in _split3(r_all))

    kscale = dh ** -0.5
    for h in range(nh):
        hs = slice(h * dh, (h + 1) * dh)
        qb = q_ref[:, hs]
        kb = k_src[:, hs]
        vb = v_src[:, hs]
        qf = qb.astype(F32)
        dm = b_all[0:lq, h:h + 1] + rrow[h * ck:h * ck + lq, :]
        dm = jnp.where(tri[0:lq, :], dm, NEG_BIG)
        inter = inter_all[0:lq, h:h + 1]
        m_row = jnp.maximum(inter, jnp.max(dm, axis=-1, keepdims=True))
        dexp = jnp.exp(dm - m_row) * kscale
        inter_w = jnp.exp(inter - m_row)
        s = lax.dot_general(qb, kb, _NT, preferred_element_type=F32) * dexp
        ch = c_s[h]
        nrow = n_s[h:h + 1, :]
        num = (jnp.dot(s.astype(BF16), vb, preferred_element_type=F32)
               + inter_w * jnp.dot(qb, ch.astype(BF16), preferred_element_type=F32))
        den = (jnp.sum(s, axis=-1, keepdims=True)
               + inter_w * jnp.sum(qf * nrow, axis=-1, keepdims=True))
        hv = num / jnp.maximum(jnp.abs(den), jnp.exp(-m_row))
        hn = _rms(hv, ng_ref[:, hs])
        mix_ref[:, hs] = (hn * jax.nn.sigmoid(o_ref[:, hs].astype(F32))).astype(BF16)

        wt = wt_all[:, h:h + 1]
        decay = decay_all[:, h:h + 1]
        wv = (wt * vb.astype(F32)).astype(BF16)
        c_s[h] = decay * ch + kscale * lax.dot_general(kb, wv, _TN, preferred_element_type=F32)
        n_s[h:h + 1, :] = decay * nrow + kscale * jnp.sum(wt * kb.astype(F32), axis=0,
                                                          keepdims=True)

    m_s[...] = m_new

    @pl.when(c == nchunk - 1)
    def _():
        c_out[...] = c_s[...]
        n_out[...] = n_s[...]
        m_out[...] = m_s[...]


def _mlstm(proj, gates, gbias, ngain, l, c0, n0, m0, ls, *, row0, nseq, seqlen):
    nh, dh = c0.shape[2], c0.shape[3]
    w = nh * dh
    lq = min(seqlen, LANE)
    nchunk = seqlen // lq
    rb0 = row0 // lq

    def rowmap(off):
        return lambda b, c: (rb0 + b * nchunk + c, off)

    pads = []
    if lq < LANE:
        pads = [pltpu.VMEM((LANE, w), BF16), pltpu.VMEM((LANE, w), BF16),
                pltpu.VMEM((LANE, 2 * LANE), F32)]
    return pl.pallas_call(
        functools.partial(_mlstm_kernel, lq=lq, nchunk=nchunk),
        out_shape=[jax.ShapeDtypeStruct((nseq * seqlen, w), BF16),
                   jax.ShapeDtypeStruct((nseq, nh, dh, dh), F32),
                   jax.ShapeDtypeStruct((nseq, nh, dh), F32),
                   jax.ShapeDtypeStruct((nseq, 1, LANE), F32)],
        grid=(nseq, nchunk),
        in_specs=[
            pl.BlockSpec((lq, w), rowmap(0)),
            pl.BlockSpec((lq, w), rowmap(1)),
            pl.BlockSpec((lq, w), rowmap(2)),
            pl.BlockSpec((lq, w), rowmap(3)),
            pl.BlockSpec((lq, 2 * LANE), rowmap(0)),
            _layer_spec((1, 2 * LANE), l),
            _layer_spec((1, w), l),
            pl.BlockSpec((None, None, nh, dh, dh), lambda b, c: (ls, b, 0, 0, 0)),
            pl.BlockSpec((None, None, nh, dh), lambda b, c: (ls, b, 0, 0)),
            pl.BlockSpec((None, None, 1, LANE), lambda b, c: (ls, b, 0, 0)),
        ],
        out_specs=[
            pl.BlockSpec((lq, w), lambda b, c: (b * nchunk + c, 0)),
            pl.BlockSpec((None, nh, dh, dh), lambda b, c: (b, 0, 0, 0)),
            pl.BlockSpec((None, nh, dh), lambda b, c: (b, 0, 0)),
            pl.BlockSpec((None, 1, LANE), lambda b, c: (b, 0, 0)),
        ],
        scratch_shapes=[pltpu.VMEM((nh, dh, dh), F32), pltpu.VMEM((nh, dh), F32),
                        pltpu.VMEM((1, LANE), F32)] + pads,
        compiler_params=_cparams("arbitrary", "arbitrary"),
        name="mlstm",
    )(proj, proj, proj, proj, gates, gbias, ngain, c0, n0, m0)


def _attn_kernel(q_ref, k_ref, v_ref, o_ref):
    d = q_ref.shape[1]
    hd = d // MEM_HEADS
    scale = hd ** -0.5
    for h in range(MEM_HEADS):
        hs = slice(h * hd, (h + 1) * hd)
        kb = k_ref[:, hs].astype(BF16)
        vb = v_ref[:, hs].astype(BF16)
        s = lax.dot_general(q_ref[:, hs], kb, _NT, preferred_element_type=F32) * scale
        p = jnp.exp(s - jnp.max(s, axis=-1, keepdims=True))
        p = p / jnp.sum(p, axis=-1, keepdims=True)
        o_ref[:, hs] = jnp.dot(p.astype(BF16), vb, preferred_element_type=F32).astype(BF16)


def _attn(q, mem_k, mem_v, l, *, row0, nseq, seqlen, rows):
    d = q.shape[1]
    nm = mem_k.shape[2]
    nt = seqlen // rows
    rb0 = row0 // rows
    kv_spec = pl.BlockSpec((None, None, nm, d), lambda b, r: (l, b, 0, 0))
    return pl.pallas_call(
        _attn_kernel,
        out_shape=jax.ShapeDtypeStruct((nseq * seqlen, d), BF16),
        grid=(nseq, nt),
        in_specs=[pl.BlockSpec((rows, d), lambda b, r: (rb0 + b * nt + r, 0)), kv_spec, kv_spec],
        out_specs=pl.BlockSpec((rows, d), lambda b, r: (b * nt + r, 0)),
        compiler_params=_cparams("parallel", "arbitrary"),
        name="mem_attn",
    )(q, mem_k, mem_v)


def _final_norm_kernel(x_ref, g_ref, o_ref):
    o_ref[...] = _rms(x_ref[...], g_ref[...])


def _final_norm(x, g, *, row0, nrows, rows=512):
    d = x.shape[1]
    rb0 = row0 // rows
    return pl.pallas_call(
        _final_norm_kernel,
        out_shape=jax.ShapeDtypeStruct((nrows, d), F32),
        grid=(nrows // rows,),
        in_specs=[pl.BlockSpec((rows, d), lambda i: (rb0 + i, 0)),
                  pl.BlockSpec((1, d), lambda i: (0, 0))],
        out_specs=pl.BlockSpec((rows, d), lambda i: (i, 0)),
        compiler_params=_cparams("parallel"),
        name="final_norm",
    )(x, g)


def kernel(x_prompt, x_sample, mem_prompt, state_pool, state_mlstm_C, state_mlstm_n, state_mlstm_m, cache_mem_k, cache_mem_v, g_ffn1, w_ffn1_in, w_ffn1_out, g_mix, w_in, pool_w, pool_scale, gmlp_v_gain, gmlp_ws, gmlp_bs, mlstm_i_bias, mlstm_f_bias, mlstm_norm_gain, w_out, g_xattn, g_mem, w_mem_q, w_mem_k, w_mem_v, w_mem_o, g_ffn2, w_ffn2_in, w_ffn2_out, g_final):
    nb, seq, d = x_prompt.shape
    db, dseq, _ = x_sample.shape
    depth = g_ffn1.shape[0]
    nh, dh = state_mlstm_C.shape[2], state_mlstm_C.shape[3]
    mw = nh * dh
    pw = pool_w.shape[1] * pool_w.shape[2]
    gh = gmlp_ws.shape[1]
    gw = gmlp_v_gain.shape[1]
    ghd = gw // gh
    nmem = mem_prompt.shape[1]
    npool = state_pool.shape[2]
    p_rows = nb * seq
    s_rows = db * dseq
    past = seq
    mix_cols = pw + 2 * gw
    main_cols = mix_cols + 4 * mw

    rows3 = lambda v: v.reshape(depth, 1, -1)
    g1, gm, gx, g2 = rows3(g_ffn1), rows3(g_mix), rows3(g_xattn), rows3(g_ffn2)

    zpad = jnp.zeros((depth, d, LANE - nh), F32)
    w_gate = jnp.concatenate([w_in[:, :, main_cols:main_cols + nh], zpad,
                              w_in[:, :, main_cols + nh:], zpad], axis=2).astype(BF16)
    bpad = jnp.zeros((depth, LANE - nh), F32)
    gbias = rows3(jnp.concatenate([mlstm_i_bias, bpad, mlstm_f_bias, bpad], axis=1))
    w_out_b, wo_b, w_in_b = w_out.astype(BF16), w_mem_o.astype(BF16), w_in.astype(BF16)
    pool_w_b = pool_w.astype(BF16)
    bs_full = jnp.repeat(jnp.swapaxes(gmlp_bs, 1, 2), ghd, axis=2)
    hist_p = jnp.zeros((1, nb, POOL_PAD, pw), F32)
    hist_s = jnp.pad(state_pool, ((0, 0), (0, 0), (POOL_PAD - npool, 0), (0, 0)))
    c0_p = jnp.zeros((1, nb, nh, dh, dh), F32)
    n0_p = jnp.zeros((1, nb, nh, dh), F32)
    m0_p = jnp.zeros((1, nb, 1, LANE), F32)
    c0_s = state_mlstm_C.astype(F32)
    n0_s = state_mlstm_n.astype(F32)
    m0_s = jnp.pad(state_mlstm_m.astype(F32), ((0, 0), (0, 0), (0, LANE - nh)))[:, :, None, :]

    mk, mv = _mem_kv(mem_prompt.reshape(nb * nmem, d), rows3(g_mem), w_mem_k, w_mem_v)
    mk = mk.reshape(depth, nb, nmem, d)
    mv = mv.reshape(depth, nb, nmem, d)

    x, xg, ssq = _prep(x_prompt.reshape(p_rows, d), x_sample.reshape(s_rows, d), g1, 0)

    outs = {k: [] for k in ("pool_p", "pool_s", "c_p", "n_p", "m_p", "c_s", "n_s", "m_s", "v_s")}
    for l in range(depth):
        act, w_down = _ffn_in(xg, ssq, w_ffn1_in, w_ffn1_out, l)
        x, xg, ssq = _mm_res(act, w_down, x, scale=0.5, g_next=(gm, l))

        proj_main, proj_mix, gates = _mix_in(xg, ssq, w_in_b, w_gate, l, n_f32_cols=mix_cols,
                                             n_cols=main_cols)

        common = (pool_w_b, rows3(pool_scale), rows3(gmlp_v_gain), gmlp_ws, bs_full, l)
        mix_a_p, pst_p = _pool_gmlp(proj_mix, hist_p, 0, *common, row0=0, nseq=nb, seqlen=seq,
                                    rows=512, pos0=0, want_v=False)
        mix_a_s, pst_s, v_rows = _pool_gmlp(proj_mix, hist_s, l, *common, row0=p_rows, nseq=db,
                                            seqlen=dseq, rows=dseq, pos0=past, want_v=True)
        outs["pool_p"].append(pst_p[:, POOL_PAD - npool:])
        outs["pool_s"].append(pst_s[:, POOL_PAD - npool:])
        outs["v_s"].append(v_rows.reshape(db, dseq, gw))

        ngain = rows3(mlstm_norm_gain)
        mix_b_p, c_p, n_p, m_p = _mlstm(proj_main, gates, gbias, ngain, l, c0_p, n0_p, m0_p, 0,
                                        row0=0, nseq=nb, seqlen=seq)
        mix_b_s, c_s, n_s, m_s = _mlstm(proj_main, gates, gbias, ngain, l, c0_s, n0_s, m0_s, l,
                                        row0=p_rows, nseq=db, seqlen=dseq)
        outs["c_p"].append(c_p)
        outs["n_p"].append(n_p)
        outs["m_p"].append(m_p[:, 0, :nh])
        outs["c_s"].append(c_s)
        outs["n_s"].append(n_s)
        outs["m_s"].append(m_s[:, 0, :nh])

        x, xg, ssq = _mm_res_groups([(mix_a_p, mix_a_s), (mix_b_p, mix_b_s)], w_out_b, l, x,
                                    (gx, l))

        q = _scaled_mm(xg, ssq, w_mem_q, l)
        att_p = _attn(q, mk, mv, l, row0=0, nseq=nb, seqlen=seq, rows=1024)
        att_s = _attn(q, cache_mem_k, cache_mem_v, l, row0=p_rows, nseq=db, seqlen=dseq,
                      rows=dseq)
        x, xg, ssq = _mm_res_groups([(att_p, att_s)], wo_b, l, x, (g2, l))

        act, w_down = _ffn_in(xg, ssq, w_ffn2_in, w_ffn2_out, l)
        if l + 1 < depth:
            x, xg, ssq = _mm_res(act, w_down, x, scale=0.5, g_next=(g1, l + 1))
        else:
            x = _mm_res(act, w_down, x, scale=0.5)

    gfin = g_final.reshape(1, d)
    y_prompt = _final_norm(x, gfin, row0=0, nrows=p_rows).reshape(nb, seq, d)
    y_sample = _final_norm(x, gfin, row0=p_rows, nrows=s_rows).reshape(db, dseq, d)
    st = lambda k: jnp.stack(outs[k])
    return (y_prompt, y_sample, st("pool_p"), st("pool_s"), st("c_p"), st("n_p"), st("m_p"),
            st("c_s"), st("n_s"), st("m_s"), st("v_s"), mk, mv)
```

```python
import functools

import jax
import jax.numpy as jnp
from jax import lax
from jax.experimental import pallas as pl
from jax.experimental.pallas import tpu as pltpu

F32 = jnp.float32
BF16 = jnp.bfloat16

EPS = 1e-6
POOL_WINDOWS = (2, 4, 8, 16)
POOL_PAD = 16
MEM_HEADS = 4
LANE = 128
VMEM_LIMIT = 56 * 1024 * 1024
NEG_BIG = -1e30

TM = 1024
TB = 2304
TBM = 1536
FFN_COLS_F32 = 256
FFN_COLS_BF16 = 512
TR = 512
MM_ROWS = 1024
MLSTM_SEQS = 4
NORM_ROWS = 256


def _cparams(*sem):
    return pltpu.CompilerParams(dimension_semantics=sem, vmem_limit_bytes=VMEM_LIMIT)


def _rms(x, g):
    ms = jnp.mean(x * x, axis=-1, keepdims=True)
    return x * lax.rsqrt(ms + EPS) * g


def _layer_spec(shape, l):
    zeros = (0,) * len(shape)
    return pl.BlockSpec((None,) + tuple(shape), lambda *_: (l,) + zeros)


def _emit_norm_inputs(x, g_ref, xg_ref, ssq_ref):
    xg_ref[...] = (x * g_ref[...]).astype(BF16)
    ssq_ref[...] = jnp.broadcast_to(jnp.sum(x * x, axis=-1, keepdims=True), ssq_ref.shape)


def _rstd(ssq_ref, d, width):
    s = ssq_ref[0]
    for p in range(1, ssq_ref.shape[0]):
        s = s + ssq_ref[p]
    r = lax.rsqrt(s / d + EPS)
    reps = width // LANE
    return r if reps == 1 else jnp.concatenate([r] * reps, axis=1)


def _ssq_spec(ssq, rows):
    return pl.BlockSpec((ssq.shape[0], rows, LANE), lambda i, *_: (0, i, 0))


def _prep_kernel(xp_ref, xs_ref, g_ref, x_ref, xg_ref, ssq_ref, *, n_prompt_tiles):
    i = pl.program_id(0)

    def emit(src):
        x = src[...]
        x_ref[...] = x
        _emit_norm_inputs(x, g_ref, xg_ref, ssq_ref)

    @pl.when(i < n_prompt_tiles)
    def _():
        emit(xp_ref)

    @pl.when(i >= n_prompt_tiles)
    def _():
        emit(xs_ref)


def _prep(xp, xs, g, l):
    p_rows, d = xp.shape
    t = p_rows + xs.shape[0]
    npt = p_rows // TR
    return pl.pallas_call(
        functools.partial(_prep_kernel, n_prompt_tiles=npt),
        out_shape=[jax.ShapeDtypeStruct((t, d), F32), jax.ShapeDtypeStruct((t, d), BF16),
                   jax.ShapeDtypeStruct((1, t, LANE), F32)],
        grid=(t // TR,),
        in_specs=[pl.BlockSpec((TR, d), lambda i: (jnp.minimum(i, npt - 1), 0)),
                  pl.BlockSpec((TR, d), lambda i: (jnp.maximum(i - npt, 0), 0)),
                  _layer_spec((1, d), l)],
        out_specs=[pl.BlockSpec((TR, d), lambda i: (i, 0)),
                   pl.BlockSpec((TR, d), lambda i: (i, 0)),
                   pl.BlockSpec((None, TR, LANE), lambda i: (0, i, 0))],
        compiler_params=_cparams("arbitrary"),
        name="prep",
    )(xp, xs, g)


def _row_chunks(rows):
    n = -(-rows // MM_ROWS)
    assert rows % n == 0
    return [slice(c * (rows // n), (c + 1) * (rows // n)) for c in range(n)]


def _ffn_in_kernel(xg_ref, ssq_ref, wg_ref, wu_ref, *rest, n_casts):
    cast_in, act_ref, cast_out = rest[:n_casts], rest[n_casts], rest[n_casts + 1:]
    wg = wg_ref[...].astype(BF16)
    wu = wu_ref[...].astype(BF16)
    for rows in _row_chunks(xg_ref.shape[0]):
        rs = _rstd(ssq_ref.at[:, rows, :], xg_ref.shape[1], act_ref.shape[1])
        xg = xg_ref[rows, :]
        gate = jnp.dot(xg, wg, preferred_element_type=F32) * rs
        up = jnp.dot(xg, wu, preferred_element_type=F32) * rs
        act_ref[rows, :] = (gate * jax.nn.sigmoid(gate) * up).astype(BF16)
    for src, dst in zip(cast_in, cast_out):
        dst[...] = src[...].astype(BF16)


def _ffn_in(xg, ssq, w, l, casts):
    t, d = xg.shape
    f = w.shape[2] // 2
    tf = FFN_COLS_BF16 if w.dtype == BF16 else FFN_COLS_F32
    nf = f // tf
    nm = t // TB
    steps = nm * nf
    cast_specs_in, cast_specs_out, cast_shapes = [], [], []
    for cw, lc in casts:
        r, cdim = cw.shape[1], cw.shape[2]
        rb = r // steps
        assert rb * steps == r and rb % 16 == 0
        cast_specs_in.append(pl.BlockSpec((None, rb, cdim),
                                          lambda i, j, lc=lc: (lc, i * nf + j, 0)))
        cast_specs_out.append(pl.BlockSpec((rb, cdim), lambda i, j: (i * nf + j, 0)))
        cast_shapes.append(jax.ShapeDtypeStruct((r, cdim), BF16))
    return pl.pallas_call(
        functools.partial(_ffn_in_kernel, n_casts=len(casts)),
        out_shape=[jax.ShapeDtypeStruct((t, f), BF16)] + cast_shapes,
        grid=(nm, nf),
        in_specs=[
            pl.BlockSpec((TB, d), lambda i, j: (i, 0)),
            _ssq_spec(ssq, TB),
            pl.BlockSpec((None, d, tf), lambda i, j: (l, 0, j)),
            pl.BlockSpec((None, d, tf), lambda i, j: (l, 0, j + nf)),
        ] + cast_specs_in,
        out_specs=[pl.BlockSpec((TB, tf), lambda i, j: (i, j))] + cast_specs_out,
        compiler_params=_cparams("arbitrary", "arbitrary"),
        name="ffn_in",
    )(xg, ssq, w, w, *[cw for cw, _ in casts])


def _mix_in_kernel(xg_ref, ssq_ref, w_ref, we_ref, main_ref, mixf_ref, gates_ref, *, n_f32_tiles):
    j = pl.program_id(1)
    d = xg_ref.shape[1]

    def project(w_ref, o_ref):
        w = w_ref[...].astype(BF16)
        for rows in _row_chunks(xg_ref.shape[0]):
            y = jnp.dot(xg_ref[rows, :], w, preferred_element_type=F32)
            o_ref[rows, :] = (y * _rstd(ssq_ref.at[:, rows, :], d, w.shape[1])).astype(o_ref.dtype)

    @pl.when(j < n_f32_tiles)
    def _():
        project(w_ref, mixf_ref)

    @pl.when(j >= n_f32_tiles)
    def _():
        project(w_ref, main_ref)

    @pl.when(j == 0)
    def _():
        project(we_ref, gates_ref)


def _mix_in(xg, ssq, w, w_gate, l, *, n_f32_cols, n_cols, tn=512):
    t, d = xg.shape
    nf32 = n_f32_cols // tn
    nt = n_cols // tn
    ne = w_gate.shape[2]
    return pl.pallas_call(
        functools.partial(_mix_in_kernel, n_f32_tiles=nf32),
        out_shape=[jax.ShapeDtypeStruct((t, n_cols - n_f32_cols), BF16),
                   jax.ShapeDtypeStruct((t, n_f32_cols), F32),
                   jax.ShapeDtypeStruct((t, ne), F32)],
        grid=(t // TBM, nt),
        in_specs=[
            pl.BlockSpec((TBM, d), lambda i, j: (i, 0)),
            _ssq_spec(ssq, TBM),
            pl.BlockSpec((None, d, tn), lambda i, j: (l, 0, j)),
            _layer_spec((d, ne), l),
        ],
        out_specs=[pl.BlockSpec((TBM, tn), lambda i, j: (i, jnp.maximum(j - nf32, 0))),
                   pl.BlockSpec((TBM, tn), lambda i, j: (i, jnp.minimum(j, nf32 - 1))),
                   pl.BlockSpec((TBM, ne), lambda i, j: (i, 0))],
        compiler_params=_cparams("arbitrary", "arbitrary"),
        name="mix_in",
    )(xg, ssq, w, w_gate)


def _scaled_mm_kernel(xg_ref, ssq_ref, w_ref, o_ref):
    w = w_ref[...].astype(BF16)
    for rows in _row_chunks(xg_ref.shape[0]):
        y = jnp.dot(xg_ref[rows, :], w, preferred_element_type=F32)
        rs = _rstd(ssq_ref.at[:, rows, :], xg_ref.shape[1], o_ref.shape[1])
        o_ref[rows, :] = (y * rs).astype(o_ref.dtype)


def _scaled_mm(xg, ssq, w, l, *, tn=512):
    t, d = xg.shape
    n = w.shape[2]
    return pl.pallas_call(
        _scaled_mm_kernel,
        out_shape=jax.ShapeDtypeStruct((t, n), BF16),
        grid=(t // TB, n // tn),
        in_specs=[pl.BlockSpec((TB, d), lambda i, j: (i, 0)),
                  _ssq_spec(ssq, TB),
                  pl.BlockSpec((None, d, tn), lambda i, j: (l, 0, j))],
        out_specs=pl.BlockSpec((TB, tn), lambda i, j: (i, j)),
        compiler_params=_cparams("parallel", "arbitrary"),
        name="scaled_mm",
    )(xg, ssq, w)


def _norm_rows_to(x_ref, g_ref, h_ref):
    rows = x_ref.shape[0]

    def body(r, carry):
        sl = pl.ds(pl.multiple_of(r * NORM_ROWS, NORM_ROWS), NORM_ROWS)
        h_ref[sl, :] = _rms(x_ref[sl, :], g_ref[...]).astype(BF16)
        return carry

    lax.fori_loop(0, rows // NORM_ROWS, body, 0)


def _mem_kv_kernel(x_ref, g_ref, wk_ref, wv_ref, ok_ref, ov_ref, h_ref):
    @pl.when(pl.program_id(1) == 0)
    def _():
        _norm_rows_to(x_ref, g_ref, h_ref)

    h = h_ref[...]
    ok_ref[...] = jnp.dot(h, wk_ref[...].astype(BF16), preferred_element_type=F32)
    ov_ref[...] = jnp.dot(h, wv_ref[...].astype(BF16), preferred_element_type=F32)


def _mem_kv(mem, g, wk, wv, *, tn=512):
    rows, d = mem.shape
    nl = g.shape[0]
    out = jax.ShapeDtypeStruct((nl, rows, d), F32)
    w_spec = pl.BlockSpec((None, d, tn), lambda ll, j: (ll, 0, j))
    o_spec = pl.BlockSpec((None, rows, tn), lambda ll, j: (ll, 0, j))
    return pl.pallas_call(
        _mem_kv_kernel,
        out_shape=[out, out],
        grid=(nl, d // tn),
        in_specs=[pl.BlockSpec((rows, d), lambda ll, j: (0, 0)),
                  pl.BlockSpec((None, 1, d), lambda ll, j: (ll, 0, 0)),
                  w_spec, w_spec],
        out_specs=[o_spec, o_spec],
        scratch_shapes=[pltpu.VMEM((rows, d), BF16)],
        compiler_params=_cparams("arbitrary", "arbitrary"),
        name="mem_kv",
    )(mem, g, wk, wv)


def _mm_res_kernel(a_ref, w_ref, r_ref, *rest, scale, nk, emit):
    if emit:
        g_ref, o_ref, xg_ref, ssq_ref = rest
    else:
        (o_ref,) = rest
    k = pl.program_id(2)

    def part():
        return scale * jnp.dot(a_ref[...], w_ref[...], preferred_element_type=F32)

    @pl.when(k == 0)
    def _():
        o_ref[...] = r_ref[...] + part()

    @pl.when(jnp.logical_and(k > 0, k < nk - 1))
    def _():
        o_ref[...] += part()

    @pl.when(k == nk - 1)
    def _():
        x = o_ref[...] + part()
        o_ref[...] = x
        if emit:
            _emit_norm_inputs(x, g_ref, xg_ref, ssq_ref)


def _mm_res(a, w, res, *, scale, g_next=None, tn=1024, tk=2048):
    t, kdim = a.shape
    n = w.shape[1]
    nk = kdim // tk
    assert nk >= 2
    emit = g_next is not None
    in_specs = [pl.BlockSpec((TM, tk), lambda i, j, k: (i, k)),
                pl.BlockSpec((tk, tn), lambda i, j, k: (k, j)),
                pl.BlockSpec((TM, tn), lambda i, j, k: (i, j))]
    out_shape = [jax.ShapeDtypeStruct((t, n), F32)]
    out_specs = [pl.BlockSpec((TM, tn), lambda i, j, k: (i, j))]
    args = [a, w, res]
    if emit:
        gains, ln = g_next
        in_specs.append(pl.BlockSpec((None, 1, tn), lambda i, j, k: (ln, 0, j)))
        args.append(gains)
        out_shape += [jax.ShapeDtypeStruct((t, n), BF16),
                      jax.ShapeDtypeStruct((n // tn, t, LANE), F32)]
        out_specs += [pl.BlockSpec((TM, tn), lambda i, j, k: (i, j)),
                      pl.BlockSpec((None, TM, LANE), lambda i, j, k: (j, i, 0))]
    res_out = pl.pallas_call(
        functools.partial(_mm_res_kernel, scale=scale, nk=nk, emit=emit),
        out_shape=out_shape,
        grid=(t // TM, n // tn, nk),
        in_specs=in_specs,
        out_specs=out_specs,
        compiler_params=_cparams("parallel", "parallel", "arbitrary"),
        name="mm_res",
    )(*args)
    return res_out if emit else res_out[0]


def _mm_res_groups_kernel(*refs, n_parts, n_prompt_tiles):
    ap = refs[0:n_parts]
    asm = refs[n_parts:2 * n_parts]
    w_ref, r_ref, g_ref, o_ref, xg_ref, ssq_ref = refs[2 * n_parts:]
    i = pl.program_id(0)

    def run(parts):
        acc = r_ref[...]
        k0 = 0
        for p in parts:
            kw = p.shape[1]
            acc = acc + jnp.dot(p[...], w_ref[k0:k0 + kw, :], preferred_element_type=F32)
            k0 += kw
        o_ref[...] = acc
        _emit_norm_inputs(acc, g_ref, xg_ref, ssq_ref)

    @pl.when(i < n_prompt_tiles)
    def _():
        run(ap)

    @pl.when(i >= n_prompt_tiles)
    def _():
        run(asm)


def _mm_res_groups(parts, w, l, res, g_next):
    t, n = res.shape
    p_rows = parts[0][0].shape[0]
    npt = p_rows // TR
    assert p_rows % TR == 0 and (t - p_rows) % TR == 0
    kdim = w.shape[1]
    gains, ln = g_next
    in_specs = ([pl.BlockSpec((TR, p[0].shape[1]), lambda i: (jnp.minimum(i, npt - 1), 0))
                 for p in parts]
                + [pl.BlockSpec((TR, p[1].shape[1]), lambda i: (jnp.maximum(i - npt, 0), 0))
                   for p in parts]
                + [_layer_spec((kdim, n), l),
                   pl.BlockSpec((TR, n), lambda i: (i, 0)),
                   _layer_spec((1, n), ln)])
    return pl.pallas_call(
        functools.partial(_mm_res_groups_kernel, n_parts=len(parts), n_prompt_tiles=npt),
        out_shape=[jax.ShapeDtypeStruct((t, n), F32), jax.ShapeDtypeStruct((t, n), BF16),
                   jax.ShapeDtypeStruct((1, t, LANE), F32)],
        grid=(t // TR,),
        in_specs=in_specs,
        out_specs=[pl.BlockSpec((TR, n), lambda i: (i, 0)),
                   pl.BlockSpec((TR, n), lambda i: (i, 0)),
                   pl.BlockSpec((None, TR, LANE), lambda i: (0, i, 0))],
        compiler_params=_cparams("arbitrary"),
        name="mm_res_groups",
    )(*[p[0] for p in parts], *[p[1] for p in parts], w, res, gains)


def _pool_gmlp_kernel(xp_ref, u_ref, v_ref, hist_ref, pw_ref, ps_ref, vg_ref, ws_ref, bs_ref,
                      mix_ref, pst_ref, *rest, rows, pos0, want_v):
    if want_v:
        vrows_ref, full_ref, vpad_ref = rest
    else:
        full_ref, vpad_ref = rest
    r = pl.program_id(1)
    cw = xp_ref.shape[1]
    ng = len(POOL_WINDOWS)
    gd = cw // ng

    @pl.when(r == 0)
    def _():
        full_ref[0:POOL_PAD, :] = hist_ref[...]

    @pl.when(r > 0)
    def _():
        full_ref[0:POOL_PAD, :] = full_ref[rows:rows + POOL_PAD, :]

    x = xp_ref[...]
    full_ref[POOL_PAD:POOL_PAD + rows, :] = x
    pos = pos0 + r * rows + lax.broadcasted_iota(jnp.int32, (rows, gd), 0)
    for g, w in enumerate(POOL_WINDOWS):
        cs = slice(g * gd, (g + 1) * gd)
        acc = x[:, cs]
        for s in range(1, w):
            acc = acc + full_ref[POOL_PAD - s:POOL_PAD - s + rows, cs]
        cnt = jnp.minimum(pos + 1, w).astype(F32)
        dg = acc / cnt - x[:, cs]
        y = jnp.dot(dg.astype(BF16), pw_ref[g], preferred_element_type=F32)
        mix_ref[:, cs] = (y * ps_ref[:, cs]).astype(BF16)
    pst_ref[...] = full_ref[rows:rows + POOL_PAD, :]

    nh = ws_ref.shape[0]
    hd = v_ref.shape[1] // nh
    ck = ws_ref.shape[1]
    lc = min(rows, ck)
    tri = (lax.broadcasted_iota(jnp.int32, (lc, ck), 1)
           <= lax.broadcasted_iota(jnp.int32, (lc, ck), 0))
    if lc < ck:
        @pl.when(jnp.logical_and(pl.program_id(0) == 0, r == 0))
        def _():
            vpad_ref[...] = jnp.zeros_like(vpad_ref)
    for h in range(nh):
        hs = slice(h * hd, (h + 1) * hd)
        wsm = jnp.where(tri, ws_ref[h, 0:lc, :], 0.0).astype(BF16)
        bias = bs_ref[0:lc, hs]
        for c in range(rows // lc):
            rs = slice(c * lc, (c + 1) * lc)
            vh = _rms(v_ref[rs, hs], vg_ref[:, hs])
            if want_v:
                vrows_ref[rs, hs] = vh
            if lc < ck:
                vpad_ref[0:lc, :] = vh.astype(BF16)
                rhs = vpad_ref[...]
            else:
                rhs = vh.astype(BF16)
            s = jnp.dot(wsm, rhs, preferred_element_type=F32) + bias
            mix_ref[rs, cw + h * hd:cw + (h + 1) * hd] = (u_ref[rs, hs] * s).astype(BF16)


def _pool_gmlp(proj, hist, lh, pool_w, pool_scale, v_gain, ws, bs_full, l, *, row0, nseq, seqlen,
               rows, pos0, want_v):
    cw = pool_w.shape[1] * pool_w.shape[2]
    gw = v_gain.shape[2]
    nt = seqlen // rows
    rb0 = row0 // rows

    def rowmap(off):
        return lambda b, r: (rb0 + b * nt + r, off)

    out_shape = [jax.ShapeDtypeStruct((nseq * seqlen, cw + gw), BF16),
                 jax.ShapeDtypeStruct((nseq, POOL_PAD, cw), F32)]
    out_specs = [pl.BlockSpec((rows, cw + gw), lambda b, r: (b * nt + r, 0)),
                 pl.BlockSpec((None, POOL_PAD, cw), lambda b, r: (b, 0, 0))]
    if want_v:
        out_shape.append(jax.ShapeDtypeStruct((nseq * seqlen, gw), F32))
        out_specs.append(pl.BlockSpec((rows, gw), lambda b, r: (b * nt + r, 0)))
    return pl.pallas_call(
        functools.partial(_pool_gmlp_kernel, rows=rows, pos0=pos0, want_v=want_v),
        out_shape=out_shape,
        grid=(nseq, nt),
        in_specs=[
            pl.BlockSpec((rows, cw), rowmap(0)),
            pl.BlockSpec((rows, gw), rowmap(1)),
            pl.BlockSpec((rows, gw), rowmap(2)),
            pl.BlockSpec((None, None, POOL_PAD, cw), lambda b, r: (lh, b, 0, 0)),
            _layer_spec(pool_w.shape[1:], l),
            _layer_spec(pool_scale.shape[1:], l),
            _layer_spec(v_gain.shape[1:], l),
            _layer_spec(ws.shape[1:], l),
            _layer_spec(bs_full.shape[1:], l),
        ],
        out_specs=out_specs,
        scratch_shapes=[pltpu.VMEM((rows + 2 * POOL_PAD, cw), F32),
                        pltpu.VMEM((ws.shape[2], gw // ws.shape[1]), BF16)],
        compiler_params=_cparams("arbitrary", "arbitrary"),
        name="pool_gmlp",
    )(proj, proj, proj, hist, pool_w, pool_scale, v_gain, ws, bs_full)


def _split3(x):
    hi = x.astype(BF16)
    r1 = x - hi.astype(F32)
    mid = r1.astype(BF16)
    lo = (r1 - mid.astype(F32)).astype(BF16)
    return hi, mid, lo


_NT = (((1,), (1,)), ((), ()))
_TN = (((0,), (0,)), ((), ()))


def _mlstm_kernel(q_ref, k_ref, v_ref, o_ref, gt_ref, gb_ref, ng_ref, c0_ref, n0_ref, m0_ref,
                  mix_ref, c_out, n_out, m_out, st_s, m_s, *pads, lq, nchunk, nsq):
    ck = LANE
    nh, dh = st_s.shape[1], st_s.shape[2]
    c = pl.program_id(1)
    eye = (lax.broadcasted_iota(jnp.int32, (dh, dh), 0)
           == lax.broadcasted_iota(jnp.int32, (dh, dh), 1))
    tri = (lax.broadcasted_iota(jnp.int32, (ck, ck), 1)
           <= lax.broadcasted_iota(jnp.int32, (ck, ck), 0))
    rowi = lax.broadcasted_iota(jnp.int32, (ck, LANE), 0)
    kscale = dh ** -0.5
    e_r = lax.broadcasted_iota(jnp.int32, (LANE, nh * dh), 0)
    e_c = lax.broadcasted_iota(jnp.int32, (LANE, nh * dh), 1)
    spread = jnp.where(e_r == lax.shift_right_logical(e_c, dh.bit_length() - 1), 1.0, 0.0)
    spread = spread.astype(BF16)
    tri_b = jnp.where(tri, 1.0, 0.0).astype(BF16)
    ones_k = jnp.ones((ck, dh), BF16)
    ones_d = jnp.ones((dh, dh), BF16)

    @pl.when(c == 0)
    def _():
        for sq in range(nsq):
            for h in range(nh):
                st_s[sq, h, :, 0:dh] = c0_ref[sq, h]
                ncol = jnp.sum(jnp.where(eye, n0_ref[sq, h:h + 1, :], 0.0), axis=1, keepdims=True)
                st_s[sq, h, :, dh:2 * dh] = jnp.broadcast_to(ncol, (dh, dh))
        m_s[...] = m0_ref[...]

    if lq < ck:
        kp_ref, vp_ref, gp_ref = pads

        @pl.when(jnp.logical_and(pl.program_id(0) == 0, c == 0))
        def _():
            kp_ref[...] = jnp.zeros_like(kp_ref)
            vp_ref[...] = jnp.zeros_like(vp_ref)
            gp_ref[...] = jnp.zeros_like(gp_ref)

    seqs = range(nsq)
    rqs = [slice(sq * lq, (sq + 1) * lq) for sq in seqs]
    if lq < ck:
        for sq in seqs:
            kp_ref[sq, 0:lq, :] = k_ref[rqs[sq], :]
            vp_ref[sq, 0:lq, :] = v_ref[rqs[sq], :]
            gp_ref[sq, 0:lq, :] = gt_ref[rqs[sq], :]
        k_src = [kp_ref.at[sq] for sq in seqs]
        v_src = [vp_ref.at[sq] for sq in seqs]
        g_src = [gp_ref.at[sq] for sq in seqs]
    else:
        k_src, v_src, g_src = [k_ref], [v_ref], [gt_ref]

    gts = [g_src[sq][...] + gb_ref[...] for sq in seqs]
    lf = [jax.nn.log_sigmoid(g[:, LANE:2 * LANE]) for g in gts]
    b_all = [sum(jnp.dot(tri_b, p, preferred_element_type=F32) for p in _split3(x)) for x in lf]
    r_all = [gts[sq][:, 0:LANE] - b_all[sq] for sq in seqs]
    pm = list(r_all)
    sh = 1
    while sh < ck:
        pm = [jnp.maximum(x, jnp.where(rowi >= sh, pltpu.roll(x, sh, axis=0), NEG_BIG)) for x in pm]
        sh *= 2
    m_all = [m_s[sq] for sq in seqs]
    big_m = [jnp.maximum(m_all[sq], pm[sq]) for sq in seqs]
    neg_m = [jnp.log(kscale) - x for x in big_m]
    iw_all = [jnp.exp(m_all[sq] - big_m[sq]) for sq in seqs]
    e_all = [jnp.exp(-(b_all[sq] + big_m[sq])) for sq in seqs]
    m_last = [x[lq - 1:lq, :] for x in big_m]
    for sq in seqs:
        m_s[sq] = b_all[sq][lq - 1:lq, :] + m_last[sq]
    decay_all = [jnp.exp(m_all[sq] - m_last[sq]) for sq in seqs]
    wt_all = [jnp.where(rowi < lq, jnp.exp(r_all[sq] - m_last[sq]), 0.0) * kscale for sq in seqs]
    iw_bc = [jnp.dot(x.astype(BF16), spread, preferred_element_type=F32) for x in iw_all]
    wt_bc = [jnp.dot(x.astype(BF16), spread, preferred_element_type=F32) for x in wt_all]
    r_rows = [x.T for x in r_all]

    units = [(sq, h, slice(h * dh, (h + 1) * dh)) for sq in seqs for h in range(nh)]
    s_raw = [lax.dot_general(q_ref[rqs[sq], hs], k_src[sq][:, hs], _NT, preferred_element_type=F32)
             for sq, h, hs in units]
    sts = [st_s[sq, h] for sq, h, hs in units]
    lhs = []
    for u, (sq, h, hs) in enumerate(units):
        dm = neg_m[sq][0:lq, h:h + 1] + r_rows[sq][h:h + 1, :]
        dexp = jnp.exp(jnp.where(tri[0:lq, :], dm, NEG_BIG))
        qi = (iw_bc[sq][0:lq, hs] * q_ref[rqs[sq], hs].astype(F32)).astype(BF16)
        lhs.append(jnp.concatenate([(s_raw[u] * dexp).astype(BF16), qi], axis=1))
    nd = [jnp.dot(lhs[u], jnp.concatenate([jnp.concatenate([v_src[sq][:, hs], ones_k], axis=1),
                                           sts[u].astype(BF16)], axis=0),
                  preferred_element_type=F32) for u, (sq, h, hs) in enumerate(units)]
    hv = [nd[u][:, 0:dh] / jnp.maximum(jnp.abs(nd[u][:, dh:2 * dh]), e_all[sq][0:lq, h:h + 1])
          for u, (sq, h, hs) in enumerate(units)]
    ms = [jnp.dot((x * x).astype(BF16), ones_d, preferred_element_type=F32) * (1.0 / dh) for x in hv]
    for u, (sq, h, hs) in enumerate(units):
        hn = hv[u] * lax.rsqrt(ms[u] + EPS) * ng_ref[:, hs]
        mix_ref[rqs[sq], hs] = (hn * jax.nn.sigmoid(o_ref[rqs[sq], hs].astype(F32))).astype(BF16)
    for u, (sq, h, hs) in enumerate(units):
        wtb = wt_bc[sq][:, hs]
        waug = jnp.concatenate([(wtb * v_src[sq][:, hs].astype(F32)).astype(BF16),
                                wtb.astype(BF16)], axis=1)
        st_s[sq, h] = (decay_all[sq][:, h:h + 1] * sts[u]
                       + lax.dot_general(k_src[sq][:, hs], waug, _TN, preferred_element_type=F32))

    @pl.when(c == nchunk - 1)
    def _():
        for sq in range(nsq):
            for h in range(nh):
                c_out[sq, h] = st_s[sq, h, :, 0:dh]
                n_out[sq, h:h + 1, :] = jnp.sum(jnp.where(eye, st_s[sq, h, :, dh:2 * dh], 0.0),
                                                axis=0, keepdims=True)
        m_out[...] = m_s[...]


def _mlstm(proj, gates, gbias, ngain, l, c0, n0, m0, ls, *, row0, nseq, seqlen):
    nh, dh = c0.shape[2], c0.shape[3]
    w = nh * dh
    lq = min(seqlen, LANE)
    nchunk = seqlen // lq
    nsq = MLSTM_SEQS if nchunk == 1 else 1
    assert nseq % nsq == 0
    rows = nsq * lq
    rb0 = row0 // rows

    def rowmap(off):
        return lambda b, c: (rb0 + b * nchunk + c, off)

    pads = []
    if lq < LANE:
        pads = [pltpu.VMEM((nsq, LANE, w), BF16), pltpu.VMEM((nsq, LANE, w), BF16),
                pltpu.VMEM((nsq, LANE, 2 * LANE), F32)]
    return pl.pallas_call(
        functools.partial(_mlstm_kernel, lq=lq, nchunk=nchunk, nsq=nsq),
        out_shape=[jax.ShapeDtypeStruct((nseq * seqlen, w), BF16),
                   jax.ShapeDtypeStruct((nseq, nh, dh, dh), F32),
                   jax.ShapeDtypeStruct((nseq, nh, dh), F32),
                   jax.ShapeDtypeStruct((nseq, 1, LANE), F32)],
        grid=(nseq // nsq, nchunk),
        in_specs=[
            pl.BlockSpec((rows, w), rowmap(0)),
            pl.BlockSpec((rows, w), rowmap(1)),
            pl.BlockSpec((rows, w), rowmap(2)),
            pl.BlockSpec((rows, w), rowmap(3)),
            pl.BlockSpec((rows, 2 * LANE), rowmap(0)),
            _layer_spec((1, 2 * LANE), l),
            _layer_spec((1, w), l),
            pl.BlockSpec((None, nsq, nh, dh, dh), lambda b, c: (ls, b, 0, 0, 0)),
            pl.BlockSpec((None, nsq, nh, dh), lambda b, c: (ls, b, 0, 0)),
            pl.BlockSpec((None, nsq, 1, LANE), lambda b, c: (ls, b, 0, 0)),
        ],
        out_specs=[
            pl.BlockSpec((rows, w), lambda b, c: (b * nchunk + c, 0)),
            pl.BlockSpec((nsq, nh, dh, dh), lambda b, c: (b, 0, 0, 0)),
            pl.BlockSpec((nsq, nh, dh), lambda b, c: (b, 0, 0)),
            pl.BlockSpec((nsq, 1, LANE), lambda b, c: (b, 0, 0)),
        ],
        scratch_shapes=[pltpu.VMEM((nsq, nh, dh, 2 * dh), F32),
                        pltpu.VMEM((nsq, 1, LANE), F32)] + pads,
        compiler_params=_cparams("arbitrary", "arbitrary"),
        name="mlstm",
    )(proj, proj, proj, proj, gates, gbias, ngain, c0, n0, m0)


def _attn_kernel(q_ref, k_ref, v_ref, o_ref):
    d = q_ref.shape[1]
    hd = d // MEM_HEADS
    scale = hd ** -0.5
    heads = [slice(h * hd, (h + 1) * hd) for h in range(MEM_HEADS)]
    kb = [k_ref[:, hs].astype(BF16) for hs in heads]
    vb = [v_ref[:, hs].astype(BF16) for hs in heads]
    s = [lax.dot_general(q_ref[:, hs], kb[h], _NT, preferred_element_type=F32) * scale
         for h, hs in enumerate(heads)]
    p = [jnp.exp(x - jnp.max(x, axis=-1, keepdims=True)) for x in s]
    p = [x / jnp.sum(x, axis=-1, keepdims=True) for x in p]
    for h, hs in enumerate(heads):
        o_ref[:, hs] = jnp.dot(p[h].astype(BF16), vb[h], preferred_element_type=F32).astype(BF16)


def _attn(q, mem_k, mem_v, l, *, row0, nseq, seqlen, rows):
    d = q.shape[1]
    nm = mem_k.shape[2]
    nt = seqlen // rows
    rb0 = row0 // rows
    kv_spec = pl.BlockSpec((None, None, nm, d), lambda b, r: (l, b, 0, 0))
    return pl.pallas_call(
        _attn_kernel,
        out_shape=jax.ShapeDtypeStruct((nseq * seqlen, d), BF16),
        grid=(nseq, nt),
        in_specs=[pl.BlockSpec((rows, d), lambda b, r: (rb0 + b * nt + r, 0)), kv_spec, kv_spec],
        out_specs=pl.BlockSpec((rows, d), lambda b, r: (b * nt + r, 0)),
        compiler_params=_cparams("parallel", "arbitrary"),
        name="mem_attn",
    )(q, mem_k, mem_v)


def _final_norm_kernel(x_ref, g_ref, o_ref):
    o_ref[...] = _rms(x_ref[...], g_ref[...])


def _final_norm(x, g, *, row0, nrows, rows=512):
    d = x.shape[1]
    rb0 = row0 // rows
    return pl.pallas_call(
        _final_norm_kernel,
        out_shape=jax.ShapeDtypeStruct((nrows, d), F32),
        grid=(nrows // rows,),
        in_specs=[pl.BlockSpec((rows, d), lambda i: (rb0 + i, 0)),
                  pl.BlockSpec((1, d), lambda i: (0, 0))],
        out_specs=pl.BlockSpec((rows, d), lambda i: (i, 0)),
        compiler_params=_cparams("parallel"),
        name="final_norm",
    )(x, g)


def kernel(x_prompt, x_sample, mem_prompt, state_pool, state_mlstm_C, state_mlstm_n, state_mlstm_m, cache_mem_k, cache_mem_v, g_ffn1, w_ffn1_in, w_ffn1_out, g_mix, w_in, pool_w, pool_scale, gmlp_v_gain, gmlp_ws, gmlp_bs, mlstm_i_bias, mlstm_f_bias, mlstm_norm_gain, w_out, g_xattn, g_mem, w_mem_q, w_mem_k, w_mem_v, w_mem_o, g_ffn2, w_ffn2_in, w_ffn2_out, g_final):
    nb, seq, d = x_prompt.shape
    db, dseq, _ = x_sample.shape
    depth = g_ffn1.shape[0]
    nh, dh = state_mlstm_C.shape[2], state_mlstm_C.shape[3]
    mw = nh * dh
    pw = pool_w.shape[1] * pool_w.shape[2]
    gh = gmlp_ws.shape[1]
    gw = gmlp_v_gain.shape[1]
    ghd = gw // gh
    nmem = mem_prompt.shape[1]
    npool = state_pool.shape[2]
    p_rows = nb * seq
    s_rows = db * dseq
    past = seq
    mix_cols = pw + 2 * gw
    main_cols = mix_cols + 4 * mw

    rows3 = lambda v: v.reshape(depth, 1, -1)
    g1, gm, gx, g2 = rows3(g_ffn1), rows3(g_mix), rows3(g_xattn), rows3(g_ffn2)

    w_in_b = w_in.astype(BF16)
    zpad = jnp.zeros((depth, d, LANE - nh), BF16)
    w_gate = jnp.concatenate([w_in_b[:, :, main_cols:main_cols + nh], zpad,
                              w_in_b[:, :, main_cols + nh:], zpad], axis=2)
    bpad = jnp.zeros((depth, LANE - nh), F32)
    gbias = rows3(jnp.concatenate([mlstm_i_bias, bpad, mlstm_f_bias, bpad], axis=1))
    pool_w_b = pool_w.astype(BF16)
    bs_full = jnp.repeat(jnp.swapaxes(gmlp_bs, 1, 2), ghd, axis=2)
    hist_p = jnp.zeros((1, nb, POOL_PAD, pw), F32)
    hist_s = jnp.pad(state_pool, ((0, 0), (0, 0), (POOL_PAD - npool, 0), (0, 0)))
    c0_p = jnp.zeros((1, nb, nh, dh, dh), F32)
    n0_p = jnp.zeros((1, nb, nh, dh), F32)
    m0_p = jnp.zeros((1, nb, 1, LANE), F32)
    c0_s = state_mlstm_C.astype(F32)
    n0_s = state_mlstm_n.astype(F32)
    m0_s = jnp.pad(state_mlstm_m.astype(F32), ((0, 0), (0, 0), (0, LANE - nh)))[:, :, None, :]

    mk, mv = _mem_kv(mem_prompt.reshape(nb * nmem, d), rows3(g_mem), w_mem_k, w_mem_v)
    mk = mk.reshape(depth, nb, nmem, d)
    mv = mv.reshape(depth, nb, nmem, d)

    x, xg, ssq = _prep(x_prompt.reshape(p_rows, d), x_sample.reshape(s_rows, d), g1, 0)

    outs = {k: [] for k in ("pool_p", "pool_s", "c_p", "n_p", "m_p", "c_s", "n_s", "m_s", "v_s")}
    w_next = None
    for l in range(depth):
        w_up, lw = (w_ffn1_in, l) if w_next is None else (w_next[None], 0)
        act, w_down, w_out_b, wo_b, w_next = _ffn_in(
            xg, ssq, w_up, lw, [(w_ffn1_out, l), (w_out, l), (w_mem_o, l), (w_ffn2_in, l)])
        x, xg, ssq = _mm_res(act, w_down, x, scale=0.5, g_next=(gm, l))

        proj_main, proj_mix, gates = _mix_in(xg, ssq, w_in_b, w_gate, l, n_f32_cols=mix_cols,
                                             n_cols=main_cols)

        common = (pool_w_b, rows3(pool_scale), rows3(gmlp_v_gain), gmlp_ws, bs_full, l)
        mix_a_p, pst_p = _pool_gmlp(proj_mix, hist_p, 0, *common, row0=0, nseq=nb, seqlen=seq,
                                    rows=512, pos0=0, want_v=False)
        mix_a_s, pst_s, v_rows = _pool_gmlp(proj_mix, hist_s, l, *common, row0=p_rows, nseq=db,
                                            seqlen=dseq, rows=dseq, pos0=past, want_v=True)
        outs["pool_p"].append(pst_p[:, POOL_PAD - npool:])
        outs["pool_s"].append(pst_s[:, POOL_PAD - npool:])
        outs["v_s"].append(v_rows.reshape(db, dseq, gw))

        ngain = rows3(mlstm_norm_gain)
        mix_b_p, c_p, n_p, m_p = _mlstm(proj_main, gates, gbias, ngain, l, c0_p, n0_p, m0_p, 0,
                                        row0=0, nseq=nb, seqlen=seq)
        mix_b_s, c_s, n_s, m_s = _mlstm(proj_main, gates, gbias, ngain, l, c0_s, n0_s, m0_s, l,
                                        row0=p_rows, nseq=db, seqlen=dseq)
        outs["c_p"].append(c_p)
        outs["n_p"].append(n_p)
        outs["m_p"].append(m_p[:, 0, :nh])
        outs["c_s"].append(c_s)
        outs["n_s"].append(n_s)
        outs["m_s"].append(m_s[:, 0, :nh])

        x, xg, ssq = _mm_res_groups([(mix_a_p, mix_a_s), (mix_b_p, mix_b_s)], w_out_b[None], 0, x,
                                    (gx, l))

        q = _scaled_mm(xg, ssq, w_mem_q, l)
        att_p = _attn(q, mk, mv, l, row0=0, nseq=nb, seqlen=seq, rows=1024)
        att_s = _attn(q, cache_mem_k, cache_mem_v, l, row0=p_rows, nseq=db, seqlen=dseq,
                      rows=dseq)
        x, xg, ssq = _mm_res_groups([(att_p, att_s)], wo_b[None], 0, x, (g2, l))

        if l + 1 < depth:
            act, w_down, w_next = _ffn_in(xg, ssq, w_next[None], 0,
                                          [(w_ffn2_out, l), (w_ffn1_in, l + 1)])
            x, xg, ssq = _mm_res(act, w_down, x, scale=0.5, g_next=(g1, l + 1))
        else:
            act, w_down = _ffn_in(xg, ssq, w_next[None], 0, [(w_ffn2_out, l)])
            x = _mm_res(act, w_down, x, scale=0.5)

    gfin = g_final.reshape(1, d)
    y_prompt = _final_norm(x, gfin, row0=0, nrows=p_rows).reshape(nb, seq, d)
    y_sample = _final_norm(x, gfin, row0=p_rows, nrows=s_rows).reshape(db, dseq, d)
    st = lambda k: jnp.stack(outs[k])
    return (y_prompt, y_sample, st("pool_p"), st("pool_s"), st("c_p"), st("n_p"), st("m_p"),
            st("c_s"), st("n_s"), st("m_s"), st("v_s"), mk, mv)
```

```python
import functools

import jax
import jax.numpy as jnp
from jax import lax
from jax.experimental import pallas as pl
from jax.experimental.pallas import tpu as pltpu

F32 = jnp.float32
BF16 = jnp.bfloat16

EPS = 1e-6
POOL_WINDOWS = (2, 4, 8, 16)
POOL_PAD = 16
MEM_HEADS = 4
LANE = 128
VMEM_LIMIT = 56 * 1024 * 1024
NEG_BIG = -1e30

TM = 1024
TB = 2304
TBM = 2304
FFN_COLS_F32 = 256
FFN_COLS_BF16 = 512
TR = 512
MM_ROWS = 1024
MLSTM_SEQS = 4
NORM_ROWS = 256


def _cparams(*sem):
    return pltpu.CompilerParams(dimension_semantics=sem, vmem_limit_bytes=VMEM_LIMIT)


def _rms(x, g):
    ms = jnp.mean(x * x, axis=-1, keepdims=True)
    return x * lax.rsqrt(ms + EPS) * g


def _layer_spec(shape, l):
    zeros = (0,) * len(shape)
    return pl.BlockSpec((None,) + tuple(shape), lambda *_: (l,) + zeros)


def _emit_norm_inputs(x, g_ref, xg_ref, ssq_ref):
    xg_ref[...] = (x * g_ref[...]).astype(BF16)
    ssq_ref[...] = jnp.broadcast_to(jnp.sum(x * x, axis=-1, keepdims=True), ssq_ref.shape)


def _rstd(ssq_ref, d, width):
    s = ssq_ref[0]
    for p in range(1, ssq_ref.shape[0]):
        s = s + ssq_ref[p]
    r = lax.rsqrt(s / d + EPS)
    reps = width // LANE
    return r if reps == 1 else jnp.concatenate([r] * reps, axis=1)


def _ssq_spec(ssq, rows):
    return pl.BlockSpec((ssq.shape[0], rows, LANE), lambda i, *_: (0, i, 0))


def _prep_kernel(xp_ref, xs_ref, g_ref, xg_ref, ssq_ref, *, n_prompt_tiles):
    i = pl.program_id(0)

    @pl.when(i < n_prompt_tiles)
    def _():
        _emit_norm_inputs(xp_ref[...], g_ref, xg_ref, ssq_ref)

    @pl.when(i >= n_prompt_tiles)
    def _():
        _emit_norm_inputs(xs_ref[...], g_ref, xg_ref, ssq_ref)


def _prep(xp, xs, g, l):
    p_rows, d = xp.shape
    t = p_rows + xs.shape[0]
    npt = p_rows // TR
    return pl.pallas_call(
        functools.partial(_prep_kernel, n_prompt_tiles=npt),
        out_shape=[jax.ShapeDtypeStruct((t, d), BF16), jax.ShapeDtypeStruct((1, t, LANE), F32)],
        grid=(t // TR,),
        in_specs=[pl.BlockSpec((TR, d), lambda i: (jnp.minimum(i, npt - 1), 0)),
                  pl.BlockSpec((TR, d), lambda i: (jnp.maximum(i - npt, 0), 0)),
                  _layer_spec((1, d), l)],
        out_specs=[pl.BlockSpec((TR, d), lambda i: (i, 0)),
                   pl.BlockSpec((None, TR, LANE), lambda i: (0, i, 0))],
        compiler_params=_cparams("arbitrary"),
        name="prep",
    )(xp, xs, g)


def _row_chunks(rows):
    n = -(-rows // MM_ROWS)
    assert rows % n == 0
    return [slice(c * (rows // n), (c + 1) * (rows // n)) for c in range(n)]


def _ffn_in_kernel(xg_ref, ssq_ref, wg_ref, wu_ref, *rest, n_casts):
    cast_in, act_ref, cast_out = rest[:n_casts], rest[n_casts], rest[n_casts + 1:]
    wg = wg_ref[...].astype(BF16)
    wu = wu_ref[...].astype(BF16)
    for rows in _row_chunks(xg_ref.shape[0]):
        rs = _rstd(ssq_ref.at[:, rows, :], xg_ref.shape[1], act_ref.shape[1])
        xg = xg_ref[rows, :]
        gate = jnp.dot(xg, wg, preferred_element_type=F32) * rs
        up = jnp.dot(xg, wu, preferred_element_type=F32) * rs
        act_ref[rows, :] = (gate * jax.nn.sigmoid(gate) * up).astype(BF16)
    for src, dst in zip(cast_in, cast_out):
        dst[...] = src[...].astype(BF16)


def _ffn_in(xg, ssq, w, l, casts):
    t, d = xg.shape
    f = w.shape[2] // 2
    tf = FFN_COLS_BF16 if w.dtype == BF16 else FFN_COLS_F32
    nf = f // tf
    nm = t // TB
    steps = nm * nf
    cast_specs_in, cast_specs_out, cast_shapes = [], [], []
    for cw, lc in casts:
        r, cdim = cw.shape[1], cw.shape[2]
        rb = r // steps
        assert rb * steps == r and rb % 16 == 0
        cast_specs_in.append(pl.BlockSpec((None, rb, cdim),
                                          lambda i, j, lc=lc: (lc, i * nf + j, 0)))
        cast_specs_out.append(pl.BlockSpec((rb, cdim), lambda i, j: (i * nf + j, 0)))
        cast_shapes.append(jax.ShapeDtypeStruct((r, cdim), BF16))
    return pl.pallas_call(
        functools.partial(_ffn_in_kernel, n_casts=len(casts)),
        out_shape=[jax.ShapeDtypeStruct((t, f), BF16)] + cast_shapes,
        grid=(nm, nf),
        in_specs=[
            pl.BlockSpec((TB, d), lambda i, j: (i, 0)),
            _ssq_spec(ssq, TB),
            pl.BlockSpec((None, d, tf), lambda i, j: (l, 0, j)),
            pl.BlockSpec((None, d, tf), lambda i, j: (l, 0, j + nf)),
        ] + cast_specs_in,
        out_specs=[pl.BlockSpec((TB, tf), lambda i, j: (i, j))] + cast_specs_out,
        compiler_params=_cparams("arbitrary", "arbitrary"),
        name="ffn_in",
    )(xg, ssq, w, w, *[cw for cw, _ in casts])


def _mix_in_kernel(xg_ref, ssq_ref, w_ref, we_ref, main_ref, mixf_ref, gates_ref, *, n_f32_tiles):
    j = pl.program_id(1)
    d = xg_ref.shape[1]

    def project(w_ref, o_ref):
        w = w_ref[...].astype(BF16)
        for rows in _row_chunks(xg_ref.shape[0]):
            y = jnp.dot(xg_ref[rows, :], w, preferred_element_type=F32)
            o_ref[rows, :] = (y * _rstd(ssq_ref.at[:, rows, :], d, w.shape[1])).astype(o_ref.dtype)

    @pl.when(j < n_f32_tiles)
    def _():
        project(w_ref, mixf_ref)

    @pl.when(j >= n_f32_tiles)
    def _():
        project(w_ref, main_ref)

    @pl.when(j == 0)
    def _():
        project(we_ref, gates_ref)


def _mix_in(xg, ssq, w, w_gate, l, *, n_f32_cols, n_cols, tn=512):
    t, d = xg.shape
    nf32 = n_f32_cols // tn
    nt = n_cols // tn
    ne = w_gate.shape[2]
    return pl.pallas_call(
        functools.partial(_mix_in_kernel, n_f32_tiles=nf32),
        out_shape=[jax.ShapeDtypeStruct((t, n_cols - n_f32_cols), BF16),
                   jax.ShapeDtypeStruct((t, n_f32_cols), F32),
                   jax.ShapeDtypeStruct((t, ne), F32)],
        grid=(t // TBM, nt),
        in_specs=[
            pl.BlockSpec((TBM, d), lambda i, j: (i, 0)),
            _ssq_spec(ssq, TBM),
            pl.BlockSpec((None, d, tn), lambda i, j: (l, 0, j)),
            _layer_spec((d, ne), l),
        ],
        out_specs=[pl.BlockSpec((TBM, tn), lambda i, j: (i, jnp.maximum(j - nf32, 0))),
                   pl.BlockSpec((TBM, tn), lambda i, j: (i, jnp.minimum(j, nf32 - 1))),
                   pl.BlockSpec((TBM, ne), lambda i, j: (i, 0))],
        compiler_params=_cparams("arbitrary", "arbitrary"),
        name="mix_in",
    )(xg, ssq, w, w_gate)


def _norm_rows_to(x_ref, g_ref, h_ref):
    rows = x_ref.shape[0]

    def body(r, carry):
        sl = pl.ds(pl.multiple_of(r * NORM_ROWS, NORM_ROWS), NORM_ROWS)
        h_ref[sl, :] = _rms(x_ref[sl, :], g_ref[...]).astype(BF16)
        return carry

    lax.fori_loop(0, rows // NORM_ROWS, body, 0)


def _mem_kv_kernel(x_ref, g_ref, wk_ref, wv_ref, ok_ref, ov_ref, h_ref):
    @pl.when(pl.program_id(1) == 0)
    def _():
        _norm_rows_to(x_ref, g_ref, h_ref)

    h = h_ref[...]
    ok_ref[...] = jnp.dot(h, wk_ref[...].astype(BF16), preferred_element_type=F32)
    ov_ref[...] = jnp.dot(h, wv_ref[...].astype(BF16), preferred_element_type=F32)


def _mem_kv(mem, g, wk, wv, *, tn=512):
    rows, d = mem.shape
    nl = g.shape[0]
    out = jax.ShapeDtypeStruct((nl, rows, d), F32)
    w_spec = pl.BlockSpec((None, d, tn), lambda ll, j: (ll, 0, j))
    o_spec = pl.BlockSpec((None, rows, tn), lambda ll, j: (ll, 0, j))
    return pl.pallas_call(
        _mem_kv_kernel,
        out_shape=[out, out],
        grid=(nl, d // tn),
        in_specs=[pl.BlockSpec((rows, d), lambda ll, j: (0, 0)),
                  pl.BlockSpec((None, 1, d), lambda ll, j: (ll, 0, 0)),
                  w_spec, w_spec],
        out_specs=[o_spec, o_spec],
        scratch_shapes=[pltpu.VMEM((rows, d), BF16)],
        compiler_params=_cparams("arbitrary", "arbitrary"),
        name="mem_kv",
    )(mem, g, wk, wv)


def _mm_res_kernel(a_ref, w_ref, *rest, scale, nk, emit, n_res, n_prompt_tiles):
    r_refs, rest = rest[:n_res], rest[n_res:]
    if emit:
        g_ref, o_ref, xg_ref, ssq_ref = rest
    else:
        (o_ref,) = rest
    i = pl.program_id(0)
    k = pl.program_id(2)

    def part():
        return scale * jnp.dot(a_ref[...], w_ref[...], preferred_element_type=F32)

    @pl.when(k == 0)
    def _():
        if n_res == 1:
            o_ref[...] = r_refs[0][...] + part()
        else:
            @pl.when(i < n_prompt_tiles)
            def _():
                o_ref[...] = r_refs[0][...] + part()

            @pl.when(i >= n_prompt_tiles)
            def _():
                o_ref[...] = r_refs[1][...] + part()

    @pl.when(jnp.logical_and(k > 0, k < nk - 1))
    def _():
        o_ref[...] += part()

    @pl.when(k == nk - 1)
    def _():
        x = o_ref[...] + part()
        o_ref[...] = x
        if emit:
            _emit_norm_inputs(x, g_ref, xg_ref, ssq_ref)


def _mm_res(a, w, res, *, scale, g_next=None, tn=1024, tk=2048):
    t, kdim = a.shape
    n = w.shape[1]
    nk = kdim // tk
    assert nk >= 2
    emit = g_next is not None
    in_specs = [pl.BlockSpec((TM, tk), lambda i, j, k: (i, k)),
                pl.BlockSpec((tk, tn), lambda i, j, k: (k, j))]
    args = [a, w]
    npt = 0
    if isinstance(res, tuple):
        npt = res[0].shape[0] // TM
        assert res[0].shape[0] == npt * TM and res[1].shape[0] == TM and (npt + 1) * TM == t
        in_specs += [pl.BlockSpec((TM, tn), lambda i, j, k: (jnp.minimum(i, npt - 1), j)),
                     pl.BlockSpec((TM, tn), lambda i, j, k: (0, j))]
        args += list(res)
    else:
        in_specs.append(pl.BlockSpec((TM, tn), lambda i, j, k: (i, j)))
        args.append(res)
    n_res = len(args) - 2
    out_shape = [jax.ShapeDtypeStruct((t, n), F32)]
    out_specs = [pl.BlockSpec((TM, tn), lambda i, j, k: (i, j))]
    if emit:
        gains, ln = g_next
        in_specs.append(pl.BlockSpec((None, 1, tn), lambda i, j, k: (ln, 0, j)))
        args.append(gains)
        out_shape += [jax.ShapeDtypeStruct((t, n), BF16),
                      jax.ShapeDtypeStruct((n // tn, t, LANE), F32)]
        out_specs += [pl.BlockSpec((TM, tn), lambda i, j, k: (i, j)),
                      pl.BlockSpec((None, TM, LANE), lambda i, j, k: (j, i, 0))]
    res_out = pl.pallas_call(
        functools.partial(_mm_res_kernel, scale=scale, nk=nk, emit=emit, n_res=n_res,
                          n_prompt_tiles=npt),
        out_shape=out_shape,
        grid=(t // TM, n // tn, nk),
        in_specs=in_specs,
        out_specs=out_specs,
        compiler_params=_cparams("arbitrary", "arbitrary", "arbitrary"),
        name="mm_res",
    )(*args)
    return res_out if emit else res_out[0]


def _mm_res_groups_kernel(*refs, n_parts, n_prompt_tiles, project):
    ap = refs[0:n_parts]
    asm = refs[n_parts:2 * n_parts]
    if project:
        w_ref, wp_ref, r_ref, g_ref, o_ref, y_ref = refs[2 * n_parts:]
    else:
        w_ref, r_ref, g_ref, o_ref, xg_ref, ssq_ref = refs[2 * n_parts:]
    i = pl.program_id(0)

    def run(parts):
        acc = r_ref[...]
        k0 = 0
        for p in parts:
            kw = p.shape[1]
            acc = acc + jnp.dot(p[...], w_ref[k0:k0 + kw, :], preferred_element_type=F32)
            k0 += kw
        o_ref[...] = acc
        if project:
            rstd = lax.rsqrt(jnp.mean(acc * acc, axis=-1, keepdims=True) + EPS)
            y = jnp.dot((acc * g_ref[...]).astype(BF16), wp_ref[...], preferred_element_type=F32)
            y_ref[...] = (y * rstd).astype(y_ref.dtype)
        else:
            _emit_norm_inputs(acc, g_ref, xg_ref, ssq_ref)

    @pl.when(i < n_prompt_tiles)
    def _():
        run(ap)

    @pl.when(i >= n_prompt_tiles)
    def _():
        run(asm)


def _mm_res_groups(parts, w, l, res, g_next, w_post=None):
    t, n = res.shape
    p_rows = parts[0][0].shape[0]
    npt = p_rows // TR
    assert p_rows % TR == 0 and (t - p_rows) % TR == 0
    kdim = w.shape[1]
    gains, ln = g_next
    project = w_post is not None
    row_spec = pl.BlockSpec((TR, n), lambda i: (i, 0))
    in_specs = ([pl.BlockSpec((TR, p[0].shape[1]), lambda i: (jnp.minimum(i, npt - 1), 0))
                 for p in parts]
                + [pl.BlockSpec((TR, p[1].shape[1]), lambda i: (jnp.maximum(i - npt, 0), 0))
                   for p in parts]
                + [_layer_spec((kdim, n), l)]
                + ([pl.BlockSpec(w_post.shape, lambda i: (0, 0))] if project else [])
                + [row_spec, _layer_spec((1, n), ln)])
    if project:
        out_shape = [jax.ShapeDtypeStruct((t, n), F32),
                     jax.ShapeDtypeStruct((t, w_post.shape[1]), BF16)]
        out_specs = [row_spec, pl.BlockSpec((TR, w_post.shape[1]), lambda i: (i, 0))]
    else:
        out_shape = [jax.ShapeDtypeStruct((t, n), F32), jax.ShapeDtypeStruct((t, n), BF16),
                     jax.ShapeDtypeStruct((1, t, LANE), F32)]
        out_specs = [row_spec, row_spec, pl.BlockSpec((None, TR, LANE), lambda i: (0, i, 0))]
    return pl.pallas_call(
        functools.partial(_mm_res_groups_kernel, n_parts=len(parts), n_prompt_tiles=npt,
                          project=project),
        out_shape=out_shape,
        grid=(t // TR,),
        in_specs=in_specs,
        out_specs=out_specs,
        compiler_params=_cparams("arbitrary"),
        name="mm_res_groups",
    )(*[p[0] for p in parts], *[p[1] for p in parts], w, *([w_post] if project else []), res,
      gains)


def _pool_gmlp_kernel(xp_ref, u_ref, v_ref, hist_ref, pw_ref, ps_ref, vg_ref, ws_ref, bs_ref,
                      mix_ref, pst_ref, *rest, rows, pos0, want_v):
    if want_v:
        vrows_ref, full_ref, vpad_ref = rest
    else:
        full_ref, vpad_ref = rest
    r = pl.program_id(1)
    cw = xp_ref.shape[1]
    ng = len(POOL_WINDOWS)
    gd = cw // ng

    @pl.when(r == 0)
    def _():
        full_ref[0:POOL_PAD, :] = hist_ref[...]

    @pl.when(r > 0)
    def _():
        full_ref[0:POOL_PAD, :] = full_ref[rows:rows + POOL_PAD, :]

    x = xp_ref[...]
    full_ref[POOL_PAD:POOL_PAD + rows, :] = x
    pos = pos0 + r * rows + lax.broadcasted_iota(jnp.int32, (rows, gd), 0)
    for g, w in enumerate(POOL_WINDOWS):
        cs = slice(g * gd, (g + 1) * gd)
        acc = x[:, cs]
        for s in range(1, w):
            acc = acc + full_ref[POOL_PAD - s:POOL_PAD - s + rows, cs]
        cnt = jnp.minimum(pos + 1, w).astype(F32)
        dg = acc / cnt - x[:, cs]
        y = jnp.dot(dg.astype(BF16), pw_ref[g], preferred_element_type=F32)
        mix_ref[:, cs] = (y * ps_ref[:, cs]).astype(BF16)
    pst_ref[...] = full_ref[rows:rows + POOL_PAD, :]

    nh = ws_ref.shape[0]
    hd = v_ref.shape[1] // nh
    ck = ws_ref.shape[1]
    lc = min(rows, ck)
    tri = (lax.broadcasted_iota(jnp.int32, (lc, ck), 1)
           <= lax.broadcasted_iota(jnp.int32, (lc, ck), 0))
    if lc < ck:
        @pl.when(jnp.logical_and(pl.program_id(0) == 0, r == 0))
        def _():
            vpad_ref[...] = jnp.zeros_like(vpad_ref)
    for h in range(nh):
        hs = slice(h * hd, (h + 1) * hd)
        wsm = jnp.where(tri, ws_ref[h, 0:lc, :], 0.0).astype(BF16)
        bias = bs_ref[0:lc, hs]
        for c in range(rows // lc):
            rs = slice(c * lc, (c + 1) * lc)
            vh = _rms(v_ref[rs, hs], vg_ref[:, hs])
            if want_v:
                vrows_ref[rs, hs] = vh
            if lc < ck:
                vpad_ref[0:lc, :] = vh.astype(BF16)
                rhs = vpad_ref[...]
            else:
                rhs = vh.astype(BF16)
            s = jnp.dot(wsm, rhs, preferred_element_type=F32) + bias
            mix_ref[rs, cw + h * hd:cw + (h + 1) * hd] = (u_ref[rs, hs] * s).astype(BF16)


def _pool_gmlp(proj, hist, lh, pool_w, pool_scale, v_gain, ws, bs_full, l, *, row0, nseq, seqlen,
               rows, pos0, want_v):
    cw = pool_w.shape[1] * pool_w.shape[2]
    gw = v_gain.shape[2]
    nt = seqlen // rows
    rb0 = row0 // rows

    def rowmap(off):
        return lambda b, r: (rb0 + b * nt + r, off)

    out_shape = [jax.ShapeDtypeStruct((nseq * seqlen, cw + gw), BF16),
                 jax.ShapeDtypeStruct((nseq, POOL_PAD, cw), F32)]
    out_specs = [pl.BlockSpec((rows, cw + gw), lambda b, r: (b * nt + r, 0)),
                 pl.BlockSpec((None, POOL_PAD, cw), lambda b, r: (b, 0, 0))]
    if want_v:
        out_shape.append(jax.ShapeDtypeStruct((nseq * seqlen, gw), F32))
        out_specs.append(pl.BlockSpec((rows, gw), lambda b, r: (b * nt + r, 0)))
    return pl.pallas_call(
        functools.partial(_pool_gmlp_kernel, rows=rows, pos0=pos0, want_v=want_v),
        out_shape=out_shape,
        grid=(nseq, nt),
        in_specs=[
            pl.BlockSpec((rows, cw), rowmap(0)),
            pl.BlockSpec((rows, gw), rowmap(1)),
            pl.BlockSpec((rows, gw), rowmap(2)),
            pl.BlockSpec((None, None, POOL_PAD, cw), lambda b, r: (lh, b, 0, 0)),
            _layer_spec(pool_w.shape[1:], l),
            _layer_spec(pool_scale.shape[1:], l),
            _layer_spec(v_gain.shape[1:], l),
            _layer_spec(ws.shape[1:], l),
            _layer_spec(bs_full.shape[1:], l),
        ],
        out_specs=out_specs,
        scratch_shapes=[pltpu.VMEM((rows + 2 * POOL_PAD, cw), F32),
                        pltpu.VMEM((ws.shape[2], gw // ws.shape[1]), BF16)],
        compiler_params=_cparams("arbitrary", "arbitrary"),
        name="pool_gmlp",
    )(proj, proj, proj, hist, pool_w, pool_scale, v_gain, ws, bs_full)


def _split3(x):
    hi = x.astype(BF16)
    r1 = x - hi.astype(F32)
    mid = r1.astype(BF16)
    lo = (r1 - mid.astype(F32)).astype(BF16)
    return hi, mid, lo


_NT = (((1,), (1,)), ((), ()))
_TN = (((0,), (0,)), ((), ()))


def _mlstm_kernel(q_ref, k_ref, v_ref, o_ref, gt_ref, gb_ref, ng_ref, c0_ref, n0_ref, m0_ref,
                  mix_ref, c_out, n_out, m_out, st_s, m_s, *pads, lq, nchunk, nsq):
    ck = LANE
    nh, dh = st_s.shape[1], st_s.shape[2]
    c = pl.program_id(1)
    eye = (lax.broadcasted_iota(jnp.int32, (dh, dh), 0)
           == lax.broadcasted_iota(jnp.int32, (dh, dh), 1))
    tri = (lax.broadcasted_iota(jnp.int32, (ck, ck), 1)
           <= lax.broadcasted_iota(jnp.int32, (ck, ck), 0))
    rowi = lax.broadcasted_iota(jnp.int32, (ck, LANE), 0)
    kscale = dh ** -0.5
    e_r = lax.broadcasted_iota(jnp.int32, (LANE, nh * dh), 0)
    e_c = lax.broadcasted_iota(jnp.int32, (LANE, nh * dh), 1)
    spread = jnp.where(e_r == lax.shift_right_logical(e_c, dh.bit_length() - 1), 1.0, 0.0)
    spread = spread.astype(BF16)
    tri_b = jnp.where(tri, 1.0, 0.0).astype(BF16)
    ones_k = jnp.ones((ck, dh), BF16)
    ones_d = jnp.ones((dh, dh), BF16)

    @pl.when(c == 0)
    def _():
        for sq in range(nsq):
            for h in range(nh):
                st_s[sq, h, :, 0:dh] = c0_ref[sq, h]
                ncol = jnp.sum(jnp.where(eye, n0_ref[sq, h:h + 1, :], 0.0), axis=1, keepdims=True)
                st_s[sq, h, :, dh:2 * dh] = jnp.broadcast_to(ncol, (dh, dh))
        m_s[...] = m0_ref[...]

    if lq < ck:
        kp_ref, vp_ref, gp_ref = pads

        @pl.when(jnp.logical_and(pl.program_id(0) == 0, c == 0))
        def _():
            kp_ref[...] = jnp.zeros_like(kp_ref)
            vp_ref[...] = jnp.zeros_like(vp_ref)
            gp_ref[...] = jnp.zeros_like(gp_ref)

    seqs = range(nsq)
    rqs = [slice(sq * lq, (sq + 1) * lq) for sq in seqs]
    if lq < ck:
        for sq in seqs:
            kp_ref[sq, 0:lq, :] = k_ref[rqs[sq], :]
            vp_ref[sq, 0:lq, :] = v_ref[rqs[sq], :]
            gp_ref[sq, 0:lq, :] = gt_ref[rqs[sq], :]
        k_src = [kp_ref.at[sq] for sq in seqs]
        v_src = [vp_ref.at[sq] for sq in seqs]
        g_src = [gp_ref.at[sq] for sq in seqs]
    else:
        k_src, v_src, g_src = [k_ref], [v_ref], [gt_ref]

    gts = [g_src[sq][...] + gb_ref[...] for sq in seqs]
    lf = [jax.nn.log_sigmoid(g[:, LANE:2 * LANE]) for g in gts]
    b_all = [sum(jnp.dot(tri_b, p, preferred_element_type=F32) for p in _split3(x)) for x in lf]
    r_all = [gts[sq][:, 0:LANE] - b_all[sq] for sq in seqs]
    pm = list(r_all)
    sh = 1
    while sh < ck:
        pm = [jnp.maximum(x, jnp.where(rowi >= sh, pltpu.roll(x, sh, axis=0), NEG_BIG)) for x in pm]
        sh *= 2
    m_all = [m_s[sq] for sq in seqs]
    big_m = [jnp.maximum(m_all[sq], pm[sq]) for sq in seqs]
    neg_m = [jnp.log(kscale) - x for x in big_m]
    iw_all = [jnp.exp(m_all[sq] - big_m[sq]) for sq in seqs]
    e_all = [jnp.exp(-(b_all[sq] + big_m[sq])) for sq in seqs]
    m_last = [x[lq - 1:lq, :] for x in big_m]
    for sq in seqs:
        m_s[sq] = b_all[sq][lq - 1:lq, :] + m_last[sq]
    decay_all = [jnp.exp(m_all[sq] - m_last[sq]) for sq in seqs]
    wt_all = [jnp.where(rowi < lq, jnp.exp(r_all[sq] - m_last[sq]), 0.0) * kscale for sq in seqs]
    iw_bc = [jnp.dot(x.astype(BF16), spread, preferred_element_type=F32) for x in iw_all]
    wt_bc = [jnp.dot(x.astype(BF16), spread, preferred_element_type=F32) for x in wt_all]
    r_rows = [x.T for x in r_all]

    units = [(sq, h, slice(h * dh, (h + 1) * dh)) for sq in seqs for h in range(nh)]
    s_raw = [lax.dot_general(q_ref[rqs[sq], hs], k_src[sq][:, hs], _NT, preferred_element_type=F32)
             for sq, h, hs in units]
    sts = [st_s[sq, h] for sq, h, hs in units]
    lhs = []
    for u, (sq, h, hs) in enumerate(units):
        dm = neg_m[sq][0:lq, h:h + 1] + r_rows[sq][h:h + 1, :]
        dexp = jnp.exp(jnp.where(tri[0:lq, :], dm, NEG_BIG))
        qi = (iw_bc[sq][0:lq, hs] * q_ref[rqs[sq], hs].astype(F32)).astype(BF16)
        lhs.append(jnp.concatenate([(s_raw[u] * dexp).astype(BF16), qi], axis=1))
    nd = [jnp.dot(lhs[u], jnp.concatenate([jnp.concatenate([v_src[sq][:, hs], ones_k], axis=1),
                                           sts[u].astype(BF16)], axis=0),
                  preferred_element_type=F32) for u, (sq, h, hs) in enumerate(units)]
    hv = [nd[u][:, 0:dh] / jnp.maximum(jnp.abs(nd[u][:, dh:2 * dh]), e_all[sq][0:lq, h:h + 1])
          for u, (sq, h, hs) in enumerate(units)]
    ms = [jnp.dot((x * x).astype(BF16), ones_d, preferred_element_type=F32) * (1.0 / dh) for x in hv]
    for u, (sq, h, hs) in enumerate(units):
        hn = hv[u] * lax.rsqrt(ms[u] + EPS) * ng_ref[:, hs]
        mix_ref[rqs[sq], hs] = (hn * jax.nn.sigmoid(o_ref[rqs[sq], hs].astype(F32))).astype(BF16)
    for u, (sq, h, hs) in enumerate(units):
        wtb = wt_bc[sq][:, hs]
        waug = jnp.concatenate([(wtb * v_src[sq][:, hs].astype(F32)).astype(BF16),
                                wtb.astype(BF16)], axis=1)
        st_s[sq, h] = (decay_all[sq][:, h:h + 1] * sts[u]
                       + lax.dot_general(k_src[sq][:, hs], waug, _TN, preferred_element_type=F32))

    @pl.when(c == nchunk - 1)
    def _():
        for sq in range(nsq):
            for h in range(nh):
                c_out[sq, h] = st_s[sq, h, :, 0:dh]
                n_out[sq, h:h + 1, :] = jnp.sum(jnp.where(eye, st_s[sq, h, :, dh:2 * dh], 0.0),
                                                axis=0, keepdims=True)
        m_out[...] = m_s[...]


def _mlstm(proj, gates, gbias, ngain, l, c0, n0, m0, ls, *, row0, nseq, seqlen):
    nh, dh = c0.shape[2], c0.shape[3]
    w = nh * dh
    lq = min(seqlen, LANE)
    nchunk = seqlen // lq
    nsq = MLSTM_SEQS if nchunk == 1 else 1
    assert nseq % nsq == 0
    rows = nsq * lq
    rb0 = row0 // rows

    def rowmap(off):
        return lambda b, c: (rb0 + b * nchunk + c, off)

    pads = []
    if lq < LANE:
        pads = [pltpu.VMEM((nsq, LANE, w), BF16), pltpu.VMEM((nsq, LANE, w), BF16),
                pltpu.VMEM((nsq, LANE, 2 * LANE), F32)]
    return pl.pallas_call(
        functools.partial(_mlstm_kernel, lq=lq, nchunk=nchunk, nsq=nsq),
        out_shape=[jax.ShapeDtypeStruct((nseq * seqlen, w), BF16),
                   jax.ShapeDtypeStruct((nseq, nh, dh, dh), F32),
                   jax.ShapeDtypeStruct((nseq, nh, dh), F32),
                   jax.ShapeDtypeStruct((nseq, 1, LANE), F32)],
        grid=(nseq // nsq, nchunk),
        in_specs=[
            pl.BlockSpec((rows, w), rowmap(0)),
            pl.BlockSpec((rows, w), rowmap(1)),
            pl.BlockSpec((rows, w), rowmap(2)),
            pl.BlockSpec((rows, w), rowmap(3)),
            pl.BlockSpec((rows, 2 * LANE), rowmap(0)),
            _layer_spec((1, 2 * LANE), l),
            _layer_spec((1, w), l),
            pl.BlockSpec((None, nsq, nh, dh, dh), lambda b, c: (ls, b, 0, 0, 0)),
            pl.BlockSpec((None, nsq, nh, dh), lambda b, c: (ls, b, 0, 0)),
            pl.BlockSpec((None, nsq, 1, LANE), lambda b, c: (ls, b, 0, 0)),
        ],
        out_specs=[
            pl.BlockSpec((rows, w), lambda b, c: (b * nchunk + c, 0)),
            pl.BlockSpec((nsq, nh, dh, dh), lambda b, c: (b, 0, 0, 0)),
            pl.BlockSpec((nsq, nh, dh), lambda b, c: (b, 0, 0)),
            pl.BlockSpec((nsq, 1, LANE), lambda b, c: (b, 0, 0)),
        ],
        scratch_shapes=[pltpu.VMEM((nsq, nh, dh, 2 * dh), F32),
                        pltpu.VMEM((nsq, 1, LANE), F32)] + pads,
        compiler_params=_cparams("arbitrary", "arbitrary"),
        name="mlstm",
    )(proj, proj, proj, proj, gates, gbias, ngain, c0, n0, m0)


def _attn_kernel(q_ref, k_ref, v_ref, o_ref):
    d = q_ref.shape[1]
    hd = d // MEM_HEADS
    scale = hd ** -0.5
    heads = [slice(h * hd, (h + 1) * hd) for h in range(MEM_HEADS)]
    kb = [k_ref[:, hs].astype(BF16) for hs in heads]
    vb = [v_ref[:, hs].astype(BF16) for hs in heads]
    s = [lax.dot_general(q_ref[:, hs], kb[h], _NT, preferred_element_type=F32) * scale
         for h, hs in enumerate(heads)]
    p = [jnp.exp(x - jnp.max(x, axis=-1, keepdims=True)) for x in s]
    p = [x / jnp.sum(x, axis=-1, keepdims=True) for x in p]
    for h, hs in enumerate(heads):
        o_ref[:, hs] = jnp.dot(p[h].astype(BF16), vb[h], preferred_element_type=F32).astype(BF16)


def _attn(q, mem_k, mem_v, l, *, row0, nseq, seqlen, rows):
    d = q.shape[1]
    nm = mem_k.shape[2]
    nt = seqlen // rows
    rb0 = row0 // rows
    kv_spec = pl.BlockSpec((None, None, nm, d), lambda b, r: (l, b, 0, 0))
    return pl.pallas_call(
        _attn_kernel,
        out_shape=jax.ShapeDtypeStruct((nseq * seqlen, d), BF16),
        grid=(nseq, nt),
        in_specs=[pl.BlockSpec((rows, d), lambda b, r: (rb0 + b * nt + r, 0)), kv_spec, kv_spec],
        out_specs=pl.BlockSpec((rows, d), lambda b, r: (b * nt + r, 0)),
        compiler_params=_cparams("parallel", "arbitrary"),
        name="mem_attn",
    )(q, mem_k, mem_v)


def _final_norm_kernel(x_ref, g_ref, o_ref):
    o_ref[...] = _rms(x_ref[...], g_ref[...])


def _final_norm(x, g, *, row0, nrows, rows=512):
    d = x.shape[1]
    rb0 = row0 // rows
    return pl.pallas_call(
        _final_norm_kernel,
        out_shape=jax.ShapeDtypeStruct((nrows, d), F32),
        grid=(nrows // rows,),
        in_specs=[pl.BlockSpec((rows, d), lambda i: (rb0 + i, 0)),
                  pl.BlockSpec((1, d), lambda i: (0, 0))],
        out_specs=pl.BlockSpec((rows, d), lambda i: (i, 0)),
        compiler_params=_cparams("parallel"),
        name="final_norm",
    )(x, g)


def kernel(x_prompt, x_sample, mem_prompt, state_pool, state_mlstm_C, state_mlstm_n, state_mlstm_m, cache_mem_k, cache_mem_v, g_ffn1, w_ffn1_in, w_ffn1_out, g_mix, w_in, pool_w, pool_scale, gmlp_v_gain, gmlp_ws, gmlp_bs, mlstm_i_bias, mlstm_f_bias, mlstm_norm_gain, w_out, g_xattn, g_mem, w_mem_q, w_mem_k, w_mem_v, w_mem_o, g_ffn2, w_ffn2_in, w_ffn2_out, g_final):
    nb, seq, d = x_prompt.shape
    db, dseq, _ = x_sample.shape
    depth = g_ffn1.shape[0]
    nh, dh = state_mlstm_C.shape[2], state_mlstm_C.shape[3]
    mw = nh * dh
    pw = pool_w.shape[1] * pool_w.shape[2]
    gh = gmlp_ws.shape[1]
    gw = gmlp_v_gain.shape[1]
    ghd = gw // gh
    nmem = mem_prompt.shape[1]
    npool = state_pool.shape[2]
    p_rows = nb * seq
    s_rows = db * dseq
    past = seq
    mix_cols = pw + 2 * gw
    main_cols = mix_cols + 4 * mw

    rows3 = lambda v: v.reshape(depth, 1, -1)
    g1, gm, gx, g2 = rows3(g_ffn1), rows3(g_mix), rows3(g_xattn), rows3(g_ffn2)

    w_in_b = w_in.astype(BF16)
    zpad = jnp.zeros((depth, d, LANE - nh), BF16)
    w_gate = jnp.concatenate([w_in_b[:, :, main_cols:main_cols + nh], zpad,
                              w_in_b[:, :, main_cols + nh:], zpad], axis=2)
    bpad = jnp.zeros((depth, LANE - nh), F32)
    gbias = rows3(jnp.concatenate([mlstm_i_bias, bpad, mlstm_f_bias, bpad], axis=1))
    pool_w_b = pool_w.astype(BF16)
    bs_full = jnp.repeat(jnp.swapaxes(gmlp_bs, 1, 2), ghd, axis=2)
    hist_p = jnp.zeros((1, nb, POOL_PAD, pw), F32)
    hist_s = jnp.pad(state_pool, ((0, 0), (0, 0), (POOL_PAD - npool, 0), (0, 0)))
    c0_p = jnp.zeros((1, nb, nh, dh, dh), F32)
    n0_p = jnp.zeros((1, nb, nh, dh), F32)
    m0_p = jnp.zeros((1, nb, 1, LANE), F32)
    c0_s = state_mlstm_C.astype(F32)
    n0_s = state_mlstm_n.astype(F32)
    m0_s = jnp.pad(state_mlstm_m.astype(F32), ((0, 0), (0, 0), (0, LANE - nh)))[:, :, None, :]

    mk, mv = _mem_kv(mem_prompt.reshape(nb * nmem, d), rows3(g_mem), w_mem_k, w_mem_v)
    mk = mk.reshape(depth, nb, nmem, d)
    mv = mv.reshape(depth, nb, nmem, d)

    x = (x_prompt.reshape(p_rows, d), x_sample.reshape(s_rows, d))
    xg, ssq = _prep(*x, g1, 0)

    outs = {k: [] for k in ("pool_p", "pool_s", "c_p", "n_p", "m_p", "c_s", "n_s", "m_s", "v_s")}
    w_next = None
    for l in range(depth):
        w_up, lw = (w_ffn1_in, l) if w_next is None else (w_next[None], 0)
        act, w_down, w_out_b, wq_b, wo_b, w_next = _ffn_in(
            xg, ssq, w_up, lw,
            [(w_ffn1_out, l), (w_out, l), (w_mem_q, l), (w_mem_o, l), (w_ffn2_in, l)])
        x, xg, ssq = _mm_res(act, w_down, x, scale=0.5, g_next=(gm, l))

        proj_main, proj_mix, gates = _mix_in(xg, ssq, w_in_b, w_gate, l, n_f32_cols=mix_cols,
                                             n_cols=main_cols)

        common = (pool_w_b, rows3(pool_scale), rows3(gmlp_v_gain), gmlp_ws, bs_full, l)
        mix_a_p, pst_p = _pool_gmlp(proj_mix, hist_p, 0, *common, row0=0, nseq=nb, seqlen=seq,
                                    rows=512, pos0=0, want_v=False)
        mix_a_s, pst_s, v_rows = _pool_gmlp(proj_mix, hist_s, l, *common, row0=p_rows, nseq=db,
                                            seqlen=dseq, rows=dseq, pos0=past, want_v=True)
        outs["pool_p"].append(pst_p[:, POOL_PAD - npool:])
        outs["pool_s"].append(pst_s[:, POOL_PAD - npool:])
        outs["v_s"].append(v_rows.reshape(db, dseq, gw))

        ngain = rows3(mlstm_norm_gain)
        mix_b_p, c_p, n_p, m_p = _mlstm(proj_main, gates, gbias, ngain, l, c0_p, n0_p, m0_p, 0,
                                        row0=0, nseq=nb, seqlen=seq)
        mix_b_s, c_s, n_s, m_s = _mlstm(proj_main, gates, gbias, ngain, l, c0_s, n0_s, m0_s, l,
                                        row0=p_rows, nseq=db, seqlen=dseq)
        outs["c_p"].append(c_p)
        outs["n_p"].append(n_p)
        outs["m_p"].append(m_p[:, 0, :nh])
        outs["c_s"].append(c_s)
        outs["n_s"].append(n_s)
        outs["m_s"].append(m_s[:, 0, :nh])

        x, q = _mm_res_groups([(mix_a_p, mix_a_s), (mix_b_p, mix_b_s)], w_out_b[None], 0, x,
                              (gx, l), w_post=wq_b)

        att_p = _attn(q, mk, mv, l, row0=0, nseq=nb, seqlen=seq, rows=1024)
        att_s = _attn(q, cache_mem_k, cache_mem_v, l, row0=p_rows, nseq=db, seqlen=dseq,
                      rows=dseq)
        x, xg, ssq = _mm_res_groups([(att_p, att_s)], wo_b[None], 0, x, (g2, l))

        if l + 1 < depth:
            act, w_down, w_next = _ffn_in(xg, ssq, w_next[None], 0,
                                          [(w_ffn2_out, l), (w_ffn1_in, l + 1)])
            x, xg, ssq = _mm_res(act, w_down, x, scale=0.5, g_next=(g1, l + 1))
        else:
            act, w_down = _ffn_in(xg, ssq, w_next[None], 0, [(w_ffn2_out, l)])
            x = _mm_res(act, w_down, x, scale=0.5)

    gfin = g_final.reshape(1, d)
    y_prompt = _final_norm(x, gfin, row0=0, nrows=p_rows).reshape(nb, seq, d)
    y_sample = _final_norm(x, gfin, row0=p_rows, nrows=s_rows).reshape(db, dseq, d)
    st = lambda k: jnp.stack(outs[k])
    return (y_prompt, y_sample, st("pool_p"), st("pool_s"), st("c_p"), st("n_p"), st("m_p"),
            st("c_s"), st("n_s"), st("m_s"), st("v_s"), mk, mv)
```

```python
import functools

import jax
import jax.numpy as jnp
from jax import lax
from jax.experimental import pallas as pl
from jax.experimental.pallas import tpu as pltpu

F32 = jnp.float32
BF16 = jnp.bfloat16

EPS = 1e-6
POOL_WINDOWS = (2, 4, 8, 16)
POOL_PAD = 16
MEM_HEADS = 4
LANE = 128
VMEM_LIMIT = 56 * 1024 * 1024
NEG_BIG = -1e30

TM = 1024
TB = 2304
TBM = 2304
FFN_COLS_F32 = 256
FFN_COLS_BF16 = 512
TR = 512
MM_ROWS = 1024
MLSTM_SEQS = 4
NORM_ROWS = 256


def _cparams(*sem):
    return pltpu.CompilerParams(dimension_semantics=sem, vmem_limit_bytes=VMEM_LIMIT)


def _rms(x, g):
    ms = jnp.mean(x * x, axis=-1, keepdims=True)
    return x * lax.rsqrt(ms + EPS) * g


def _layer_spec(shape, l):
    zeros = (0,) * len(shape)
    return pl.BlockSpec((None,) + tuple(shape), lambda *_: (l,) + zeros)


def _emit_norm_inputs(x, g_ref, xg_ref, ssq_ref):
    xg_ref[...] = (x * g_ref[...]).astype(BF16)
    ssq_ref[...] = jnp.broadcast_to(jnp.sum(x * x, axis=-1, keepdims=True), ssq_ref.shape)


def _rstd(ssq_ref, d, width):
    s = ssq_ref[0]
    for p in range(1, ssq_ref.shape[0]):
        s = s + ssq_ref[p]
    r = lax.rsqrt(s / d + EPS)
    reps = width // LANE
    return r if reps == 1 else jnp.concatenate([r] * reps, axis=1)


def _ssq_spec(ssq, rows):
    return pl.BlockSpec((ssq.shape[0], rows, LANE), lambda i, *_: (0, i, 0))


def _prep_kernel(xp_ref, xs_ref, g_ref, xg_ref, ssq_ref, *, n_prompt_tiles):
    i = pl.program_id(0)

    @pl.when(i < n_prompt_tiles)
    def _():
        _emit_norm_inputs(xp_ref[...], g_ref, xg_ref, ssq_ref)

    @pl.when(i >= n_prompt_tiles)
    def _():
        _emit_norm_inputs(xs_ref[...], g_ref, xg_ref, ssq_ref)


def _prep(xp, xs, g, l):
    p_rows, d = xp.shape
    t = p_rows + xs.shape[0]
    npt = p_rows // TR
    return pl.pallas_call(
        functools.partial(_prep_kernel, n_prompt_tiles=npt),
        out_shape=[jax.ShapeDtypeStruct((t, d), BF16), jax.ShapeDtypeStruct((1, t, LANE), F32)],
        grid=(t // TR,),
        in_specs=[pl.BlockSpec((TR, d), lambda i: (jnp.minimum(i, npt - 1), 0)),
                  pl.BlockSpec((TR, d), lambda i: (jnp.maximum(i - npt, 0), 0)),
                  _layer_spec((1, d), l)],
        out_specs=[pl.BlockSpec((TR, d), lambda i: (i, 0)),
                   pl.BlockSpec((None, TR, LANE), lambda i: (0, i, 0))],
        compiler_params=_cparams("arbitrary"),
        name="prep",
    )(xp, xs, g)


def _row_chunks(rows):
    n = -(-rows // MM_ROWS)
    assert rows % n == 0
    return [slice(c * (rows // n), (c + 1) * (rows // n)) for c in range(n)]


def _ffn_in_kernel(xg_ref, ssq_ref, wg_ref, wu_ref, *rest, n_casts):
    cast_in, act_ref, cast_out = rest[:n_casts], rest[n_casts], rest[n_casts + 1:]
    wg = wg_ref[...].astype(BF16)
    wu = wu_ref[...].astype(BF16)
    for rows in _row_chunks(xg_ref.shape[0]):
        rs = _rstd(ssq_ref.at[:, rows, :], xg_ref.shape[1], act_ref.shape[1])
        xg = xg_ref[rows, :]
        gate = jnp.dot(xg, wg, preferred_element_type=F32) * rs
        up = jnp.dot(xg, wu, preferred_element_type=F32) * rs
        act_ref[rows, :] = (gate * jax.nn.sigmoid(gate) * up).astype(BF16)
    for src, dst in zip(cast_in, cast_out):
        dst[...] = src[...].astype(BF16)


def _ffn_in(xg, ssq, w, l, casts):
    t, d = xg.shape
    f = w.shape[2] // 2
    tf = FFN_COLS_BF16 if w.dtype == BF16 else FFN_COLS_F32
    nf = f // tf
    nm = t // TB
    steps = nm * nf
    cast_specs_in, cast_specs_out, cast_shapes = [], [], []
    for cw, lc in casts:
        r, cdim = cw.shape[1], cw.shape[2]
        rb = r // steps
        assert rb * steps == r and rb % 16 == 0
        cast_specs_in.append(pl.BlockSpec((None, rb, cdim),
                                          lambda i, j, lc=lc: (lc, i * nf + j, 0)))
        cast_specs_out.append(pl.BlockSpec((rb, cdim), lambda i, j: (i * nf + j, 0)))
        cast_shapes.append(jax.ShapeDtypeStruct((r, cdim), BF16))
    return pl.pallas_call(
        functools.partial(_ffn_in_kernel, n_casts=len(casts)),
        out_shape=[jax.ShapeDtypeStruct((t, f), BF16)] + cast_shapes,
        grid=(nm, nf),
        in_specs=[
            pl.BlockSpec((TB, d), lambda i, j: (i, 0)),
            _ssq_spec(ssq, TB),
            pl.BlockSpec((None, d, tf), lambda i, j: (l, 0, j)),
            pl.BlockSpec((None, d, tf), lambda i, j: (l, 0, j + nf)),
        ] + cast_specs_in,
        out_specs=[pl.BlockSpec((TB, tf), lambda i, j: (i, j))] + cast_specs_out,
        compiler_params=_cparams("arbitrary", "arbitrary"),
        name="ffn_in",
    )(xg, ssq, w, w, *[cw for cw, _ in casts])


def _mix_in_kernel(xg_ref, ssq_ref, w_ref, we_ref, main_ref, mixf_ref, gates_ref, *, n_f32_tiles):
    j = pl.program_id(1)
    d = xg_ref.shape[1]

    def project(w_ref, o_ref):
        w = w_ref[...].astype(BF16)
        for rows in _row_chunks(xg_ref.shape[0]):
            y = jnp.dot(xg_ref[rows, :], w, preferred_element_type=F32)
            o_ref[rows, :] = (y * _rstd(ssq_ref.at[:, rows, :], d, w.shape[1])).astype(o_ref.dtype)

    @pl.when(j < n_f32_tiles)
    def _():
        project(w_ref, mixf_ref)

    @pl.when(j >= n_f32_tiles)
    def _():
        project(w_ref, main_ref)

    @pl.when(j == 0)
    def _():
        project(we_ref, gates_ref)


def _mix_in(xg, ssq, w, w_gate, l, *, n_f32_cols, n_cols, tn=512):
    t, d = xg.shape
    nf32 = n_f32_cols // tn
    nt = n_cols // tn
    ne = w_gate.shape[2]
    return pl.pallas_call(
        functools.partial(_mix_in_kernel, n_f32_tiles=nf32),
        out_shape=[jax.ShapeDtypeStruct((t, n_cols - n_f32_cols), BF16),
                   jax.ShapeDtypeStruct((t, n_f32_cols), F32),
                   jax.ShapeDtypeStruct((t, ne), F32)],
        grid=(t // TBM, nt),
        in_specs=[
            pl.BlockSpec((TBM, d), lambda i, j: (i, 0)),
            _ssq_spec(ssq, TBM),
            pl.BlockSpec((None, d, tn), lambda i, j: (l, 0, j)),
            _layer_spec((d, ne), l),
        ],
        out_specs=[pl.BlockSpec((TBM, tn), lambda i, j: (i, jnp.maximum(j - nf32, 0))),
                   pl.BlockSpec((TBM, tn), lambda i, j: (i, jnp.minimum(j, nf32 - 1))),
                   pl.BlockSpec((TBM, ne), lambda i, j: (i, 0))],
        compiler_params=_cparams("arbitrary", "arbitrary"),
        name="mix_in",
    )(xg, ssq, w, w_gate)


def _norm_rows_to(x_ref, g_ref, h_ref):
    rows = x_ref.shape[0]

    def body(r, carry):
        sl = pl.ds(pl.multiple_of(r * NORM_ROWS, NORM_ROWS), NORM_ROWS)
        h_ref[sl, :] = _rms(x_ref[sl, :], g_ref[...]).astype(BF16)
        return carry

    lax.fori_loop(0, rows // NORM_ROWS, body, 0)


def _mem_kv_kernel(x_ref, g_ref, wk_ref, wv_ref, ok_ref, ov_ref, h_ref):
    @pl.when(pl.program_id(1) == 0)
    def _():
        _norm_rows_to(x_ref, g_ref, h_ref)

    h = h_ref[...]
    ok_ref[...] = jnp.dot(h, wk_ref[...].astype(BF16), preferred_element_type=F32)
    ov_ref[...] = jnp.dot(h, wv_ref[...].astype(BF16), preferred_element_type=F32)


def _mem_kv(mem, g, wk, wv, *, tn=512):
    rows, d = mem.shape
    nl = g.shape[0]
    out = jax.ShapeDtypeStruct((nl, rows, d), F32)
    w_spec = pl.BlockSpec((None, d, tn), lambda ll, j: (ll, 0, j))
    o_spec = pl.BlockSpec((None, rows, tn), lambda ll, j: (ll, 0, j))
    return pl.pallas_call(
        _mem_kv_kernel,
        out_shape=[out, out],
        grid=(nl, d // tn),
        in_specs=[pl.BlockSpec((rows, d), lambda ll, j: (0, 0)),
                  pl.BlockSpec((None, 1, d), lambda ll, j: (ll, 0, 0)),
                  w_spec, w_spec],
        out_specs=[o_spec, o_spec],
        scratch_shapes=[pltpu.VMEM((rows, d), BF16)],
        compiler_params=_cparams("arbitrary", "arbitrary"),
        name="mem_kv",
    )(mem, g, wk, wv)


def _mm_res_kernel(a_ref, w_ref, *rest, scale, nk, emit, n_res, n_prompt_tiles):
    r_refs, rest = rest[:n_res], rest[n_res:]
    if emit:
        g_ref, o_ref, xg_ref, ssq_ref = rest
    else:
        (o_ref,) = rest
    i = pl.program_id(0)
    k = pl.program_id(2)

    chunks = _row_chunks(a_ref.shape[0])

    def part(rows):
        return scale * jnp.dot(a_ref[rows, :], w_ref[...], preferred_element_type=F32)

    def first(r_ref):
        for rows in chunks:
            o_ref[rows, :] = r_ref[rows, :] + part(rows)

    @pl.when(k == 0)
    def _():
        if n_res == 1:
            first(r_refs[0])
        else:
            @pl.when(i < n_prompt_tiles)
            def _():
                first(r_refs[0])

            @pl.when(i >= n_prompt_tiles)
            def _():
                first(r_refs[1])

    @pl.when(jnp.logical_and(k > 0, k < nk - 1))
    def _():
        for rows in chunks:
            o_ref[rows, :] += part(rows)

    @pl.when(k == nk - 1)
    def _():
        for rows in chunks:
            x = o_ref[rows, :] + part(rows)
            o_ref[rows, :] = x
            if emit:
                _emit_norm_inputs(x, g_ref, xg_ref.at[rows, :], ssq_ref.at[rows, :])


def _mm_res(a, w, res, *, scale, g_next=None, tm=TM, tn=1024, tk=2048):
    t, kdim = a.shape
    n = w.shape[1]
    nk = kdim // tk
    assert nk >= 2
    emit = g_next is not None
    in_specs = [pl.BlockSpec((tm, tk), lambda i, j, k: (i, k)),
                pl.BlockSpec((tk, tn), lambda i, j, k: (k, j))]
    args = [a, w]
    npt = 0
    if isinstance(res, tuple):
        npt = res[0].shape[0] // tm
        assert res[0].shape[0] == npt * tm and res[1].shape[0] == tm and (npt + 1) * tm == t
        in_specs += [pl.BlockSpec((tm, tn), lambda i, j, k: (jnp.minimum(i, npt - 1), j)),
                     pl.BlockSpec((tm, tn), lambda i, j, k: (0, j))]
        args += list(res)
    else:
        in_specs.append(pl.BlockSpec((tm, tn), lambda i, j, k: (i, j)))
        args.append(res)
    n_res = len(args) - 2
    out_shape = [jax.ShapeDtypeStruct((t, n), F32)]
    out_specs = [pl.BlockSpec((tm, tn), lambda i, j, k: (i, j))]
    if emit:
        gains, ln = g_next
        in_specs.append(pl.BlockSpec((None, 1, tn), lambda i, j, k: (ln, 0, j)))
        args.append(gains)
        out_shape += [jax.ShapeDtypeStruct((t, n), BF16),
                      jax.ShapeDtypeStruct((n // tn, t, LANE), F32)]
        out_specs += [pl.BlockSpec((tm, tn), lambda i, j, k: (i, j)),
                      pl.BlockSpec((None, tm, LANE), lambda i, j, k: (j, i, 0))]
    res_out = pl.pallas_call(
        functools.partial(_mm_res_kernel, scale=scale, nk=nk, emit=emit, n_res=n_res,
                          n_prompt_tiles=npt),
        out_shape=out_shape,
        grid=(t // tm, n // tn, nk),
        in_specs=in_specs,
        out_specs=out_specs,
        compiler_params=_cparams("arbitrary", "arbitrary", "arbitrary"),
        name="mm_res",
    )(*args)
    return res_out if emit else res_out[0]


def _mm_res_groups_kernel(*refs, n_parts, n_prompt_tiles, project):
    ap = refs[0:n_parts]
    asm = refs[n_parts:2 * n_parts]
    if project:
        w_ref, wp_ref, r_ref, g_ref, o_ref, y_ref = refs[2 * n_parts:]
    else:
        w_ref, r_ref, g_ref, o_ref, xg_ref, ssq_ref = refs[2 * n_parts:]
    i = pl.program_id(0)

    def run(parts):
        acc = r_ref[...]
        k0 = 0
        for p in parts:
            kw = p.shape[1]
            acc = acc + jnp.dot(p[...], w_ref[k0:k0 + kw, :], preferred_element_type=F32)
            k0 += kw
        o_ref[...] = acc
        if project:
            rstd = lax.rsqrt(jnp.mean(acc * acc, axis=-1, keepdims=True) + EPS)
            y = jnp.dot((acc * g_ref[...]).astype(BF16), wp_ref[...], preferred_element_type=F32)
            y_ref[...] = (y * rstd).astype(y_ref.dtype)
        else:
            _emit_norm_inputs(acc, g_ref, xg_ref, ssq_ref)

    @pl.when(i < n_prompt_tiles)
    def _():
        run(ap)

    @pl.when(i >= n_prompt_tiles)
    def _():
        run(asm)


def _mm_res_groups(parts, w, l, res, g_next, w_post=None):
    t, n = res.shape
    p_rows = parts[0][0].shape[0]
    npt = p_rows // TR
    assert p_rows % TR == 0 and (t - p_rows) % TR == 0
    kdim = w.shape[1]
    gains, ln = g_next
    project = w_post is not None
    row_spec = pl.BlockSpec((TR, n), lambda i: (i, 0))
    in_specs = ([pl.BlockSpec((TR, p[0].shape[1]), lambda i: (jnp.minimum(i, npt - 1), 0))
                 for p in parts]
                + [pl.BlockSpec((TR, p[1].shape[1]), lambda i: (jnp.maximum(i - npt, 0), 0))
                   for p in parts]
                + [_layer_spec((kdim, n), l)]
                + ([pl.BlockSpec(w_post.shape, lambda i: (0, 0))] if project else [])
                + [row_spec, _layer_spec((1, n), ln)])
    if project:
        out_shape = [jax.ShapeDtypeStruct((t, n), F32),
                     jax.ShapeDtypeStruct((t, w_post.shape[1]), BF16)]
        out_specs = [row_spec, pl.BlockSpec((TR, w_post.shape[1]), lambda i: (i, 0))]
    else:
        out_shape = [jax.ShapeDtypeStruct((t, n), F32), jax.ShapeDtypeStruct((t, n), BF16),
                     jax.ShapeDtypeStruct((1, t, LANE), F32)]
        out_specs = [row_spec, row_spec, pl.BlockSpec((None, TR, LANE), lambda i: (0, i, 0))]
    return pl.pallas_call(
        functools.partial(_mm_res_groups_kernel, n_parts=len(parts), n_prompt_tiles=npt,
                          project=project),
        out_shape=out_shape,
        grid=(t // TR,),
        in_specs=in_specs,
        out_specs=out_specs,
        compiler_params=_cparams("arbitrary"),
        name="mm_res_groups",
    )(*[p[0] for p in parts], *[p[1] for p in parts], w, *([w_post] if project else []), res,
      gains)


def _pool_gmlp_kernel(xp_ref, u_ref, v_ref, hist_ref, pw_ref, ps_ref, vg_ref, ws_ref, bs_ref,
                      mix_ref, pst_ref, *rest, rows, pos0, want_v):
    if want_v:
        vrows_ref, full_ref, vpad_ref = rest
    else:
        full_ref, vpad_ref = rest
    r = pl.program_id(1)
    cw = xp_ref.shape[1]
    ng = len(POOL_WINDOWS)
    gd = cw // ng

    @pl.when(r == 0)
    def _():
        full_ref[0:POOL_PAD, :] = hist_ref[...]

    @pl.when(r > 0)
    def _():
        full_ref[0:POOL_PAD, :] = full_ref[rows:rows + POOL_PAD, :]

    x = xp_ref[...]
    full_ref[POOL_PAD:POOL_PAD + rows, :] = x
    pos = pos0 + r * rows + lax.broadcasted_iota(jnp.int32, (rows, gd), 0)
    for g, w in enumerate(POOL_WINDOWS):
        cs = slice(g * gd, (g + 1) * gd)
        acc = x[:, cs]
        for s in range(1, w):
            acc = acc + full_ref[POOL_PAD - s:POOL_PAD - s + rows, cs]
        cnt = jnp.minimum(pos + 1, w).astype(F32)
        dg = acc / cnt - x[:, cs]
        y = jnp.dot(dg.astype(BF16), pw_ref[g], preferred_element_type=F32)
        mix_ref[:, cs] = (y * ps_ref[:, cs]).astype(BF16)
    pst_ref[...] = full_ref[rows:rows + POOL_PAD, :]

    nh = ws_ref.shape[0]
    hd = v_ref.shape[1] // nh
    ck = ws_ref.shape[1]
    lc = min(rows, ck)
    tri = (lax.broadcasted_iota(jnp.int32, (lc, ck), 1)
           <= lax.broadcasted_iota(jnp.int32, (lc, ck), 0))
    if lc < ck:
        @pl.when(jnp.logical_and(pl.program_id(0) == 0, r == 0))
        def _():
            vpad_ref[...] = jnp.zeros_like(vpad_ref)
    for h in range(nh):
        hs = slice(h * hd, (h + 1) * hd)
        wsm = jnp.where(tri, ws_ref[h, 0:lc, :], 0.0).astype(BF16)
        bias = bs_ref[0:lc, hs]
        for c in range(rows // lc):
            rs = slice(c * lc, (c + 1) * lc)
            vh = _rms(v_ref[rs, hs], vg_ref[:, hs])
            if want_v:
                vrows_ref[rs, hs] = vh
            if lc < ck:
                vpad_ref[0:lc, :] = vh.astype(BF16)
                rhs = vpad_ref[...]
            else:
                rhs = vh.astype(BF16)
            s = jnp.dot(wsm, rhs, preferred_element_type=F32) + bias
            mix_ref[rs, cw + h * hd:cw + (h + 1) * hd] = (u_ref[rs, hs] * s).astype(BF16)


def _pool_gmlp(proj, hist, lh, pool_w, pool_scale, v_gain, ws, bs_full, l, *, row0, nseq, seqlen,
               rows, pos0, want_v):
    cw = pool_w.shape[1] * pool_w.shape[2]
    gw = v_gain.shape[2]
    nt = seqlen // rows
    rb0 = row0 // rows

    def rowmap(off):
        return lambda b, r: (rb0 + b * nt + r, off)

    out_shape = [jax.ShapeDtypeStruct((nseq * seqlen, cw + gw), BF16),
                 jax.ShapeDtypeStruct((nseq, POOL_PAD, cw), F32)]
    out_specs = [pl.BlockSpec((rows, cw + gw), lambda b, r: (b * nt + r, 0)),
                 pl.BlockSpec((None, POOL_PAD, cw), lambda b, r: (b, 0, 0))]
    if want_v:
        out_shape.append(jax.ShapeDtypeStruct((nseq * seqlen, gw), F32))
        out_specs.append(pl.BlockSpec((rows, gw), lambda b, r: (b * nt + r, 0)))
    return pl.pallas_call(
        functools.partial(_pool_gmlp_kernel, rows=rows, pos0=pos0, want_v=want_v),
        out_shape=out_shape,
        grid=(nseq, nt),
        in_specs=[
            pl.BlockSpec((rows, cw), rowmap(0)),
            pl.BlockSpec((rows, gw), rowmap(1)),
            pl.BlockSpec((rows, gw), rowmap(2)),
            pl.BlockSpec((None, None, POOL_PAD, cw), lambda b, r: (lh, b, 0, 0)),
            _layer_spec(pool_w.shape[1:], l),
            _layer_spec(pool_scale.shape[1:], l),
            _layer_spec(v_gain.shape[1:], l),
            _layer_spec(ws.shape[1:], l),
            _layer_spec(bs_full.shape[1:], l),
        ],
        out_specs=out_specs,
        scratch_shapes=[pltpu.VMEM((rows + 2 * POOL_PAD, cw), F32),
                        pltpu.VMEM((ws.shape[2], gw // ws.shape[1]), BF16)],
        compiler_params=_cparams("arbitrary", "arbitrary"),
        name="pool_gmlp",
    )(proj, proj, proj, hist, pool_w, pool_scale, v_gain, ws, bs_full)


def _split3(x):
    hi = x.astype(BF16)
    r1 = x - hi.astype(F32)
    mid = r1.astype(BF16)
    lo = (r1 - mid.astype(F32)).astype(BF16)
    return hi, mid, lo


_NT = (((1,), (1,)), ((), ()))
_TN = (((0,), (0,)), ((), ()))


def _mlstm_kernel(q_ref, k_ref, v_ref, o_ref, gt_ref, gb_ref, ng_ref, c0_ref, n0_ref, m0_ref,
                  mix_ref, c_out, n_out, m_out, st_s, m_s, *pads, lq, nchunk, nsq):
    ck = LANE
    nh, dh = st_s.shape[1], st_s.shape[2]
    c = pl.program_id(1)
    eye = (lax.broadcasted_iota(jnp.int32, (dh, dh), 0)
           == lax.broadcasted_iota(jnp.int32, (dh, dh), 1))
    tri = (lax.broadcasted_iota(jnp.int32, (ck, ck), 1)
           <= lax.broadcasted_iota(jnp.int32, (ck, ck), 0))
    rowi = lax.broadcasted_iota(jnp.int32, (ck, LANE), 0)
    kscale = dh ** -0.5
    e_r = lax.broadcasted_iota(jnp.int32, (LANE, nh * dh), 0)
    e_c = lax.broadcasted_iota(jnp.int32, (LANE, nh * dh), 1)
    spread = jnp.where(e_r == lax.shift_right_logical(e_c, dh.bit_length() - 1), 1.0, 0.0)
    spread = spread.astype(BF16)
    tri_b = jnp.where(tri, 1.0, 0.0).astype(BF16)
    ones_k = jnp.ones((ck, dh), BF16)
    ones_d = jnp.ones((dh, dh), BF16)

    @pl.when(c == 0)
    def _():
        for sq in range(nsq):
            for h in range(nh):
                st_s[sq, h, :, 0:dh] = c0_ref[sq, h]
                ncol = jnp.sum(jnp.where(eye, n0_ref[sq, h:h + 1, :], 0.0), axis=1, keepdims=True)
                st_s[sq, h, :, dh:2 * dh] = jnp.broadcast_to(ncol, (dh, dh))
        m_s[...] = m0_ref[...]

    if lq < ck:
        kp_ref, vp_ref, gp_ref = pads

        @pl.when(jnp.logical_and(pl.program_id(0) == 0, c == 0))
        def _():
            kp_ref[...] = jnp.zeros_like(kp_ref)
            vp_ref[...] = jnp.zeros_like(vp_ref)
            gp_ref[...] = jnp.zeros_like(gp_ref)

    seqs = range(nsq)
    rqs = [slice(sq * lq, (sq + 1) * lq) for sq in seqs]
    if lq < ck:
        for sq in seqs:
            kp_ref[sq, 0:lq, :] = k_ref[rqs[sq], :]
            vp_ref[sq, 0:lq, :] = v_ref[rqs[sq], :]
            gp_ref[sq, 0:lq, :] = gt_ref[rqs[sq], :]
        k_src = [kp_ref.at[sq] for sq in seqs]
        v_src = [vp_ref.at[sq] for sq in seqs]
        g_src = [gp_ref.at[sq] for sq in seqs]
    else:
        k_src, v_src, g_src = [k_ref], [v_ref], [gt_ref]

    gts = [g_src[sq][...] + gb_ref[...] for sq in seqs]
    lf = [jax.nn.log_sigmoid(g[:, LANE:2 * LANE]) for g in gts]
    b_all = [sum(jnp.dot(tri_b, p, preferred_element_type=F32) for p in _split3(x)) for x in lf]
    r_all = [gts[sq][:, 0:LANE] - b_all[sq] for sq in seqs]
    pm = list(r_all)
    sh = 1
    while sh < ck:
        pm = [jnp.maximum(x, jnp.where(rowi >= sh, pltpu.roll(x, sh, axis=0), NEG_BIG)) for x in pm]
        sh *= 2
    m_all = [m_s[sq] for sq in seqs]
    big_m = [jnp.maximum(m_all[sq], pm[sq]) for sq in seqs]
    neg_m = [jnp.log(kscale) - x for x in big_m]
    iw_all = [jnp.exp(m_all[sq] - big_m[sq]) for sq in seqs]
    e_all = [jnp.exp(-(b_all[sq] + big_m[sq])) for sq in seqs]
    m_last = [x[lq - 1:lq, :] for x in big_m]
    for sq in seqs:
        m_s[sq] = b_all[sq][lq - 1:lq, :] + m_last[sq]
    decay_all = [jnp.exp(m_all[sq] - m_last[sq]) for sq in seqs]
    wt_all = [jnp.where(rowi < lq, jnp.exp(r_all[sq] - m_last[sq]), 0.0) * kscale for sq in seqs]
    iw_bc = [jnp.dot(x.astype(BF16), spread, preferred_element_type=F32) for x in iw_all]
    wt_bc = [jnp.dot(x.astype(BF16), spread, preferred_element_type=F32) for x in wt_all]
    r_rows = [x.T for x in r_all]

    units = [(sq, h, slice(h * dh, (h + 1) * dh)) for sq in seqs for h in range(nh)]
    s_raw = [lax.dot_general(q_ref[rqs[sq], hs], k_src[sq][:, hs], _NT, preferred_element_type=F32)
             for sq, h, hs in units]
    sts = [st_s[sq, h] for sq, h, hs in units]
    lhs = []
    for u, (sq, h, hs) in enumerate(units):
        dm = neg_m[sq][0:lq, h:h + 1] + r_rows[sq][h:h + 1, :]
        dexp = jnp.exp(jnp.where(tri[0:lq, :], dm, NEG_BIG))
        qi = (iw_bc[sq][0:lq, hs] * q_ref[rqs[sq], hs].astype(F32)).astype(BF16)
        lhs.append(jnp.concatenate([(s_raw[u] * dexp).astype(BF16), qi], axis=1))
    nd = [jnp.dot(lhs[u], jnp.concatenate([jnp.concatenate([v_src[sq][:, hs], ones_k], axis=1),
                                           sts[u].astype(BF16)], axis=0),
                  preferred_element_type=F32) for u, (sq, h, hs) in enumerate(units)]
    hv = [nd[u][:, 0:dh] / jnp.maximum(jnp.abs(nd[u][:, dh:2 * dh]), e_all[sq][0:lq, h:h + 1])
          for u, (sq, h, hs) in enumerate(units)]
    ms = [jnp.dot((x * x).astype(BF16), ones_d, preferred_element_type=F32) * (1.0 / dh) for x in hv]
    for u, (sq, h, hs) in enumerate(units):
        hn = hv[u] * lax.rsqrt(ms[u] + EPS) * ng_ref[:, hs]
        mix_ref[rqs[sq], hs] = (hn * jax.nn.sigmoid(o_ref[rqs[sq], hs].astype(F32))).astype(BF16)
    for u, (sq, h, hs) in enumerate(units):
        wtb = wt_bc[sq][:, hs]
        waug = jnp.concatenate([(wtb * v_src[sq][:, hs].astype(F32)).astype(BF16),
                                wtb.astype(BF16)], axis=1)
        st_s[sq, h] = (decay_all[sq][:, h:h + 1] * sts[u]
                       + lax.dot_general(k_src[sq][:, hs], waug, _TN, preferred_element_type=F32))

    @pl.when(c == nchunk - 1)
    def _():
        for sq in range(nsq):
            for h in range(nh):
                c_out[sq, h] = st_s[sq, h, :, 0:dh]
                n_out[sq, h:h + 1, :] = jnp.sum(jnp.where(eye, st_s[sq, h, :, dh:2 * dh], 0.0),
                                                axis=0, keepdims=True)
        m_out[...] = m_s[...]


def _mlstm(proj, gates, gbias, ngain, l, c0, n0, m0, ls, *, row0, nseq, seqlen):
    nh, dh = c0.shape[2], c0.shape[3]
    w = nh * dh
    lq = min(seqlen, LANE)
    nchunk = seqlen // lq
    nsq = MLSTM_SEQS if nchunk == 1 else 1
    assert nseq % nsq == 0
    rows = nsq * lq
    rb0 = row0 // rows

    def rowmap(off):
        return lambda b, c: (rb0 + b * nchunk + c, off)

    pads = []
    if lq < LANE:
        pads = [pltpu.VMEM((nsq, LANE, w), BF16), pltpu.VMEM((nsq, LANE, w), BF16),
                pltpu.VMEM((nsq, LANE, 2 * LANE), F32)]
    return pl.pallas_call(
        functools.partial(_mlstm_kernel, lq=lq, nchunk=nchunk, nsq=nsq),
        out_shape=[jax.ShapeDtypeStruct((nseq * seqlen, w), BF16),
                   jax.ShapeDtypeStruct((nseq, nh, dh, dh), F32),
                   jax.ShapeDtypeStruct((nseq, nh, dh), F32),
                   jax.ShapeDtypeStruct((nseq, 1, LANE), F32)],
        grid=(nseq // nsq, nchunk),
        in_specs=[
            pl.BlockSpec((rows, w), rowmap(0)),
            pl.BlockSpec((rows, w), rowmap(1)),
            pl.BlockSpec((rows, w), rowmap(2)),
            pl.BlockSpec((rows, w), rowmap(3)),
            pl.BlockSpec((rows, 2 * LANE), rowmap(0)),
            _layer_spec((1, 2 * LANE), l),
            _layer_spec((1, w), l),
            pl.BlockSpec((None, nsq, nh, dh, dh), lambda b, c: (ls, b, 0, 0, 0)),
            pl.BlockSpec((None, nsq, nh, dh), lambda b, c: (ls, b, 0, 0)),
            pl.BlockSpec((None, nsq, 1, LANE), lambda b, c: (ls, b, 0, 0)),
        ],
        out_specs=[
            pl.BlockSpec((rows, w), lambda b, c: (b * nchunk + c, 0)),
            pl.BlockSpec((nsq, nh, dh, dh), lambda b, c: (b, 0, 0, 0)),
            pl.BlockSpec((nsq, nh, dh), lambda b, c: (b, 0, 0)),
            pl.BlockSpec((nsq, 1, LANE), lambda b, c: (b, 0, 0)),
        ],
        scratch_shapes=[pltpu.VMEM((nsq, nh, dh, 2 * dh), F32),
                        pltpu.VMEM((nsq, 1, LANE), F32)] + pads,
        compiler_params=_cparams("arbitrary", "arbitrary"),
        name="mlstm",
    )(proj, proj, proj, proj, gates, gbias, ngain, c0, n0, m0)


def _attn_kernel(q_ref, k_ref, v_ref, o_ref):
    d = q_ref.shape[1]
    hd = d // MEM_HEADS
    scale = hd ** -0.5
    heads = [slice(h * hd, (h + 1) * hd) for h in range(MEM_HEADS)]
    kb = [k_ref[:, hs].astype(BF16) for hs in heads]
    vb = [v_ref[:, hs].astype(BF16) for hs in heads]
    s = [lax.dot_general(q_ref[:, hs], kb[h], _NT, preferred_element_type=F32) * scale
         for h, hs in enumerate(heads)]
    p = [jnp.exp(x - jnp.max(x, axis=-1, keepdims=True)) for x in s]
    p = [x / jnp.sum(x, axis=-1, keepdims=True) for x in p]
    for h, hs in enumerate(heads):
        o_ref[:, hs] = jnp.dot(p[h].astype(BF16), vb[h], preferred_element_type=F32).astype(BF16)


def _attn(q, mem_k, mem_v, l, *, row0, nseq, seqlen, rows):
    d = q.shape[1]
    nm = mem_k.shape[2]
    nt = seqlen // rows
    rb0 = row0 // rows
    kv_spec = pl.BlockSpec((None, None, nm, d), lambda b, r: (l, b, 0, 0))
    return pl.pallas_call(
        _attn_kernel,
        out_shape=jax.ShapeDtypeStruct((nseq * seqlen, d), BF16),
        grid=(nseq, nt),
        in_specs=[pl.BlockSpec((rows, d), lambda b, r: (rb0 + b * nt + r, 0)), kv_spec, kv_spec],
        out_specs=pl.BlockSpec((rows, d), lambda b, r: (b * nt + r, 0)),
        compiler_params=_cparams("parallel", "arbitrary"),
        name="mem_attn",
    )(q, mem_k, mem_v)


def _final_norm_kernel(x_ref, g_ref, o_ref):
    o_ref[...] = _rms(x_ref[...], g_ref[...])


def _final_norm(x, g, *, row0, nrows, rows=512):
    d = x.shape[1]
    rb0 = row0 // rows
    return pl.pallas_call(
        _final_norm_kernel,
        out_shape=jax.ShapeDtypeStruct((nrows, d), F32),
        grid=(nrows // rows,),
        in_specs=[pl.BlockSpec((rows, d), lambda i: (rb0 + i, 0)),
                  pl.BlockSpec((1, d), lambda i: (0, 0))],
        out_specs=pl.BlockSpec((rows, d), lambda i: (i, 0)),
        compiler_params=_cparams("parallel"),
        name="final_norm",
    )(x, g)


def kernel(x_prompt, x_sample, mem_prompt, state_pool, state_mlstm_C, state_mlstm_n, state_mlstm_m, cache_mem_k, cache_mem_v, g_ffn1, w_ffn1_in, w_ffn1_out, g_mix, w_in, pool_w, pool_scale, gmlp_v_gain, gmlp_ws, gmlp_bs, mlstm_i_bias, mlstm_f_bias, mlstm_norm_gain, w_out, g_xattn, g_mem, w_mem_q, w_mem_k, w_mem_v, w_mem_o, g_ffn2, w_ffn2_in, w_ffn2_out, g_final):
    nb, seq, d = x_prompt.shape
    db, dseq, _ = x_sample.shape
    depth = g_ffn1.shape[0]
    nh, dh = state_mlstm_C.shape[2], state_mlstm_C.shape[3]
    mw = nh * dh
    pw = pool_w.shape[1] * pool_w.shape[2]
    gh = gmlp_ws.shape[1]
    gw = gmlp_v_gain.shape[1]
    ghd = gw // gh
    nmem = mem_prompt.shape[1]
    npool = state_pool.shape[2]
    p_rows = nb * seq
    s_rows = db * dseq
    past = seq
    mix_cols = pw + 2 * gw
    main_cols = mix_cols + 4 * mw

    rows3 = lambda v: v.reshape(depth, 1, -1)
    g1, gm, gx, g2 = rows3(g_ffn1), rows3(g_mix), rows3(g_xattn), rows3(g_ffn2)

    w_in_b = w_in.astype(BF16)
    zpad = jnp.zeros((depth, d, LANE - nh), BF16)
    w_gate = jnp.concatenate([w_in_b[:, :, main_cols:main_cols + nh], zpad,
                              w_in_b[:, :, main_cols + nh:], zpad], axis=2)
    bpad = jnp.zeros((depth, LANE - nh), F32)
    gbias = rows3(jnp.concatenate([mlstm_i_bias, bpad, mlstm_f_bias, bpad], axis=1))
    pool_w_b = pool_w.astype(BF16)
    bs_full = jnp.repeat(jnp.swapaxes(gmlp_bs, 1, 2), ghd, axis=2)
    hist_p = jnp.zeros((1, nb, POOL_PAD, pw), F32)
    hist_s = jnp.pad(state_pool, ((0, 0), (0, 0), (POOL_PAD - npool, 0), (0, 0)))
    c0_p = jnp.zeros((1, nb, nh, dh, dh), F32)
    n0_p = jnp.zeros((1, nb, nh, dh), F32)
    m0_p = jnp.zeros((1, nb, 1, LANE), F32)
    c0_s = state_mlstm_C.astype(F32)
    n0_s = state_mlstm_n.astype(F32)
    m0_s = jnp.pad(state_mlstm_m.astype(F32), ((0, 0), (0, 0), (0, LANE - nh)))[:, :, None, :]

    mk, mv = _mem_kv(mem_prompt.reshape(nb * nmem, d), rows3(g_mem), w_mem_k, w_mem_v)
    mk = mk.reshape(depth, nb, nmem, d)
    mv = mv.reshape(depth, nb, nmem, d)

    x = (x_prompt.reshape(p_rows, d), x_sample.reshape(s_rows, d))
    xg, ssq = _prep(*x, g1, 0)

    outs = {k: [] for k in ("pool_p", "pool_s", "c_p", "n_p", "m_p", "c_s", "n_s", "m_s", "v_s")}
    mm_tiles = dict(tm=1536, tk=2048)
    w_next = None
    for l in range(depth):
        w_up, lw = (w_ffn1_in, l) if w_next is None else (w_next[None], 0)
        act, w_down, w_out_b, wq_b, wo_b, w_next = _ffn_in(
            xg, ssq, w_up, lw,
            [(w_ffn1_out, l), (w_out, l), (w_mem_q, l), (w_mem_o, l), (w_ffn2_in, l)])
        x, xg, ssq = _mm_res(act, w_down, x, scale=0.5, g_next=(gm, l), **(mm_tiles if l else {}))

        proj_main, proj_mix, gates = _mix_in(xg, ssq, w_in_b, w_gate, l, n_f32_cols=mix_cols,
                                             n_cols=main_cols)

        common = (pool_w_b, rows3(pool_scale), rows3(gmlp_v_gain), gmlp_ws, bs_full, l)
        mix_a_p, pst_p = _pool_gmlp(proj_mix, hist_p, 0, *common, row0=0, nseq=nb, seqlen=seq,
                                    rows=1024, pos0=0, want_v=False)
        mix_a_s, pst_s, v_rows = _pool_gmlp(proj_mix, hist_s, l, *common, row0=p_rows, nseq=db,
                                            seqlen=dseq, rows=dseq, pos0=past, want_v=True)
        outs["pool_p"].append(pst_p[:, POOL_PAD - npool:])
        outs["pool_s"].append(pst_s[:, POOL_PAD - npool:])
        outs["v_s"].append(v_rows.reshape(db, dseq, gw))

        ngain = rows3(mlstm_norm_gain)
        mix_b_p, c_p, n_p, m_p = _mlstm(proj_main, gates, gbias, ngain, l, c0_p, n0_p, m0_p, 0,
                                        row0=0, nseq=nb, seqlen=seq)
        mix_b_s, c_s, n_s, m_s = _mlstm(proj_main, gates, gbias, ngain, l, c0_s, n0_s, m0_s, l,
                                        row0=p_rows, nseq=db, seqlen=dseq)
        outs["c_p"].append(c_p)
        outs["n_p"].append(n_p)
        outs["m_p"].append(m_p[:, 0, :nh])
        outs["c_s"].append(c_s)
        outs["n_s"].append(n_s)
        outs["m_s"].append(m_s[:, 0, :nh])

        x, q = _mm_res_groups([(mix_a_p, mix_a_s), (mix_b_p, mix_b_s)], w_out_b[None], 0, x,
                              (gx, l), w_post=wq_b)

        att_p = _attn(q, mk, mv, l, row0=0, nseq=nb, seqlen=seq, rows=1024)
        att_s = _attn(q, cache_mem_k, cache_mem_v, l, row0=p_rows, nseq=db, seqlen=dseq,
                      rows=dseq)
        x, xg, ssq = _mm_res_groups([(att_p, att_s)], wo_b[None], 0, x, (g2, l))

        if l + 1 < depth:
            act, w_down, w_next = _ffn_in(xg, ssq, w_next[None], 0,
                                          [(w_ffn2_out, l), (w_ffn1_in, l + 1)])
            x, xg, ssq = _mm_res(act, w_down, x, scale=0.5, g_next=(g1, l + 1), **mm_tiles)
        else:
            act, w_down = _ffn_in(xg, ssq, w_next[None], 0, [(w_ffn2_out, l)])
            x = _mm_res(act, w_down, x, scale=0.5, **mm_tiles)

    gfin = g_final.reshape(1, d)
    y_prompt = _final_norm(x, gfin, row0=0, nrows=p_rows).reshape(nb, seq, d)
    y_sample = _final_norm(x, gfin, row0=p_rows, nrows=s_rows).reshape(db, dseq, d)
    st = lambda k: jnp.stack(outs[k])
    return (y_prompt, y_sample, st("pool_p"), st("pool_s"), st("c_p"), st("n_p"), st("m_p"),
            st("c_s"), st("n_s"), st("m_s"), st("v_s"), mk, mv)
```

```python
import functools

import jax
import jax.numpy as jnp
from jax import lax
from jax.experimental import pallas as pl
from jax.experimental.pallas import tpu as pltpu

F32 = jnp.float32
BF16 = jnp.bfloat16

EPS = 1e-6
POOL_WINDOWS = (2, 4, 8, 16)
POOL_PAD = 16
MEM_HEADS = 4
LANE = 128
VMEM_LIMIT = 56 * 1024 * 1024
NEG_BIG = -1e30

TM = 1024
TB = 2304
TBM = 2304
FFN_COLS_F32 = 256
FFN_COLS_BF16 = 512
TR = 512
MM_ROWS = 1024
MLSTM_SEQS = 4
SHORT_SEQS = 8
NORM_ROWS = 256


def _cparams(*sem):
    return pltpu.CompilerParams(dimension_semantics=sem, vmem_limit_bytes=VMEM_LIMIT)


def _rms(x, g):
    ms = jnp.mean(x * x, axis=-1, keepdims=True)
    return x * lax.rsqrt(ms + EPS) * g


def _layer_spec(shape, l):
    zeros = (0,) * len(shape)
    return pl.BlockSpec((None,) + tuple(shape), lambda *_: (l,) + zeros)


def _emit_norm_inputs(x, g_ref, xg_ref, ssq_ref):
    xg_ref[...] = (x * g_ref[...]).astype(BF16)
    ssq_ref[...] = jnp.broadcast_to(jnp.sum(x * x, axis=-1, keepdims=True), ssq_ref.shape)


def _rstd(ssq_ref, d, width):
    s = ssq_ref[0]
    for p in range(1, ssq_ref.shape[0]):
        s = s + ssq_ref[p]
    r = lax.rsqrt(s / d + EPS)
    reps = width // LANE
    return r if reps == 1 else jnp.concatenate([r] * reps, axis=1)


def _ssq_spec(ssq, rows):
    return pl.BlockSpec((ssq.shape[0], rows, LANE), lambda i, *_: (0, i, 0))


def _prep_kernel(xp_ref, xs_ref, g_ref, xg_ref, ssq_ref, *, n_prompt_tiles):
    i = pl.program_id(0)

    @pl.when(i < n_prompt_tiles)
    def _():
        _emit_norm_inputs(xp_ref[...], g_ref, xg_ref, ssq_ref)

    @pl.when(i >= n_prompt_tiles)
    def _():
        _emit_norm_inputs(xs_ref[...], g_ref, xg_ref, ssq_ref)


def _prep(xp, xs, g, l):
    p_rows, d = xp.shape
    t = p_rows + xs.shape[0]
    npt = p_rows // TR
    return pl.pallas_call(
        functools.partial(_prep_kernel, n_prompt_tiles=npt),
        out_shape=[jax.ShapeDtypeStruct((t, d), BF16), jax.ShapeDtypeStruct((1, t, LANE), F32)],
        grid=(t // TR,),
        in_specs=[pl.BlockSpec((TR, d), lambda i: (jnp.minimum(i, npt - 1), 0)),
                  pl.BlockSpec((TR, d), lambda i: (jnp.maximum(i - npt, 0), 0)),
                  _layer_spec((1, d), l)],
        out_specs=[pl.BlockSpec((TR, d), lambda i: (i, 0)),
                   pl.BlockSpec((None, TR, LANE), lambda i: (0, i, 0))],
        compiler_params=_cparams("arbitrary"),
        name="prep",
    )(xp, xs, g)


def _row_chunks(rows):
    n = -(-rows // MM_ROWS)
    assert rows % n == 0
    return [slice(c * (rows // n), (c + 1) * (rows // n)) for c in range(n)]


def _ffn_in_kernel(xg_ref, ssq_ref, wg_ref, wu_ref, *rest, n_casts):
    cast_in, act_ref, cast_out = rest[:n_casts], rest[n_casts], rest[n_casts + 1:]
    wg = wg_ref[...].astype(BF16)
    wu = wu_ref[...].astype(BF16)
    for rows in _row_chunks(xg_ref.shape[0]):
        rs = _rstd(ssq_ref.at[:, rows, :], xg_ref.shape[1], act_ref.shape[1])
        xg = xg_ref[rows, :]
        gate = jnp.dot(xg, wg, preferred_element_type=F32) * rs
        up = jnp.dot(xg, wu, preferred_element_type=F32) * rs
        act_ref[rows, :] = (gate * jax.nn.sigmoid(gate) * up).astype(BF16)
    for src, dst in zip(cast_in, cast_out):
        dst[...] = src[...].astype(BF16)


def _ffn_in(xg, ssq, w, l, casts):
    t, d = xg.shape
    f = w.shape[2] // 2
    tf = FFN_COLS_BF16 if w.dtype == BF16 else FFN_COLS_F32
    nf = f // tf
    nm = t // TB
    steps = nm * nf
    cast_specs_in, cast_specs_out, cast_shapes = [], [], []
    for cw, lc in casts:
        r, cdim = cw.shape[1], cw.shape[2]
        rb = r // steps
        assert rb * steps == r and rb % 16 == 0
        cast_specs_in.append(pl.BlockSpec((None, rb, cdim),
                                          lambda i, j, lc=lc: (lc, i * nf + j, 0)))
        cast_specs_out.append(pl.BlockSpec((rb, cdim), lambda i, j: (i * nf + j, 0)))
        cast_shapes.append(jax.ShapeDtypeStruct((r, cdim), BF16))
    return pl.pallas_call(
        functools.partial(_ffn_in_kernel, n_casts=len(casts)),
        out_shape=[jax.ShapeDtypeStruct((t, f), BF16)] + cast_shapes,
        grid=(nm, nf),
        in_specs=[
            pl.BlockSpec((TB, d), lambda i, j: (i, 0)),
            _ssq_spec(ssq, TB),
            pl.BlockSpec((None, d, tf), lambda i, j: (l, 0, j)),
            pl.BlockSpec((None, d, tf), lambda i, j: (l, 0, j + nf)),
        ] + cast_specs_in,
        out_specs=[pl.BlockSpec((TB, tf), lambda i, j: (i, j))] + cast_specs_out,
        compiler_params=_cparams("arbitrary", "arbitrary"),
        name="ffn_in",
    )(xg, ssq, w, w, *[cw for cw, _ in casts])


def _mix_in_kernel(xg_ref, ssq_ref, w_ref, we_ref, main_ref, mixf_ref, gates_ref, *, n_f32_tiles):
    j = pl.program_id(1)
    d = xg_ref.shape[1]

    def project(w_ref, o_ref):
        w = w_ref[...].astype(BF16)
        for rows in _row_chunks(xg_ref.shape[0]):
            y = jnp.dot(xg_ref[rows, :], w, preferred_element_type=F32)
            o_ref[rows, :] = (y * _rstd(ssq_ref.at[:, rows, :], d, w.shape[1])).astype(o_ref.dtype)

    @pl.when(j < n_f32_tiles)
    def _():
        project(w_ref, mixf_ref)

    @pl.when(j >= n_f32_tiles)
    def _():
        project(w_ref, main_ref)

    @pl.when(j == 0)
    def _():
        project(we_ref, gates_ref)


def _mix_in(xg, ssq, w, w_gate, l, *, n_f32_cols, n_cols, tn=512):
    t, d = xg.shape
    nf32 = n_f32_cols // tn
    nt = n_cols // tn
    ne = w_gate.shape[2]
    return pl.pallas_call(
        functools.partial(_mix_in_kernel, n_f32_tiles=nf32),
        out_shape=[jax.ShapeDtypeStruct((t, n_cols - n_f32_cols), BF16),
                   jax.ShapeDtypeStruct((t, n_f32_cols), F32),
                   jax.ShapeDtypeStruct((t, ne), F32)],
        grid=(t // TBM, nt),
        in_specs=[
            pl.BlockSpec((TBM, d), lambda i, j: (i, 0)),
            _ssq_spec(ssq, TBM),
            pl.BlockSpec((None, d, tn), lambda i, j: (l, 0, j)),
            _layer_spec((d, ne), l),
        ],
        out_specs=[pl.BlockSpec((TBM, tn), lambda i, j: (i, jnp.maximum(j - nf32, 0))),
                   pl.BlockSpec((TBM, tn), lambda i, j: (i, jnp.minimum(j, nf32 - 1))),
                   pl.BlockSpec((TBM, ne), lambda i, j: (i, 0))],
        compiler_params=_cparams("arbitrary", "arbitrary"),
        name="mix_in",
    )(xg, ssq, w, w_gate)


def _norm_rows_to(x_ref, g_ref, h_ref):
    rows = x_ref.shape[0]

    def body(r, carry):
        sl = pl.ds(pl.multiple_of(r * NORM_ROWS, NORM_ROWS), NORM_ROWS)
        h_ref[sl, :] = _rms(x_ref[sl, :], g_ref[...]).astype(BF16)
        return carry

    lax.fori_loop(0, rows // NORM_ROWS, body, 0)


def _mem_kv_kernel(x_ref, g_ref, wk_ref, wv_ref, ok_ref, ov_ref, h_ref):
    @pl.when(pl.program_id(1) == 0)
    def _():
        _norm_rows_to(x_ref, g_ref, h_ref)

    h = h_ref[...]
    ok_ref[...] = jnp.dot(h, wk_ref[...].astype(BF16), preferred_element_type=F32)
    ov_ref[...] = jnp.dot(h, wv_ref[...].astype(BF16), preferred_element_type=F32)


def _mem_kv(mem, g, wk, wv, *, tn=512):
    rows, d = mem.shape
    nl = g.shape[0]
    out = jax.ShapeDtypeStruct((nl, rows, d), F32)
    w_spec = pl.BlockSpec((None, d, tn), lambda ll, j: (ll, 0, j))
    o_spec = pl.BlockSpec((None, rows, tn), lambda ll, j: (ll, 0, j))
    return pl.pallas_call(
        _mem_kv_kernel,
        out_shape=[out, out],
        grid=(nl, d // tn),
        in_specs=[pl.BlockSpec((rows, d), lambda ll, j: (0, 0)),
                  pl.BlockSpec((None, 1, d), lambda ll, j: (ll, 0, 0)),
                  w_spec, w_spec],
        out_specs=[o_spec, o_spec],
        scratch_shapes=[pltpu.VMEM((rows, d), BF16)],
        compiler_params=_cparams("arbitrary", "arbitrary"),
        name="mem_kv",
    )(mem, g, wk, wv)


def _mm_res_kernel(a_ref, w_ref, *rest, scale, nk, emit, n_res, n_prompt_tiles):
    r_refs, rest = rest[:n_res], rest[n_res:]
    if emit:
        g_ref, o_ref, xg_ref, ssq_ref = rest
    else:
        (o_ref,) = rest
    i = pl.program_id(0)
    k = pl.program_id(2)

    chunks = _row_chunks(a_ref.shape[0])

    def part(rows):
        return scale * jnp.dot(a_ref[rows, :], w_ref[...], preferred_element_type=F32)

    def first(r_ref):
        for rows in chunks:
            o_ref[rows, :] = r_ref[rows, :] + part(rows)

    @pl.when(k == 0)
    def _():
        if n_res == 1:
            first(r_refs[0])
        else:
            @pl.when(i < n_prompt_tiles)
            def _():
                first(r_refs[0])

            @pl.when(i >= n_prompt_tiles)
            def _():
                first(r_refs[1])

    @pl.when(jnp.logical_and(k > 0, k < nk - 1))
    def _():
        for rows in chunks:
            o_ref[rows, :] += part(rows)

    @pl.when(k == nk - 1)
    def _():
        for rows in chunks:
            x = o_ref[rows, :] + part(rows)
            o_ref[rows, :] = x
            if emit:
                _emit_norm_inputs(x, g_ref, xg_ref.at[rows, :], ssq_ref.at[rows, :])


def _mm_res(a, w, res, *, scale, g_next=None, tm=TM, tn=1024, tk=2048):
    t, kdim = a.shape
    n = w.shape[1]
    nk = kdim // tk
    assert nk >= 2
    emit = g_next is not None
    in_specs = [pl.BlockSpec((tm, tk), lambda i, j, k: (i, k)),
                pl.BlockSpec((tk, tn), lambda i, j, k: (k, j))]
    args = [a, w]
    npt = 0
    if isinstance(res, tuple):
        npt = res[0].shape[0] // tm
        assert res[0].shape[0] == npt * tm and res[1].shape[0] == tm and (npt + 1) * tm == t
        in_specs += [pl.BlockSpec((tm, tn), lambda i, j, k: (jnp.minimum(i, npt - 1), j)),
                     pl.BlockSpec((tm, tn), lambda i, j, k: (0, j))]
        args += list(res)
    else:
        in_specs.append(pl.BlockSpec((tm, tn), lambda i, j, k: (i, j)))
        args.append(res)
    n_res = len(args) - 2
    out_shape = [jax.ShapeDtypeStruct((t, n), F32)]
    out_specs = [pl.BlockSpec((tm, tn), lambda i, j, k: (i, j))]
    if emit:
        gains, ln = g_next
        in_specs.append(pl.BlockSpec((None, 1, tn), lambda i, j, k: (ln, 0, j)))
        args.append(gains)
        out_shape += [jax.ShapeDtypeStruct((t, n), BF16),
                      jax.ShapeDtypeStruct((n // tn, t, LANE), F32)]
        out_specs += [pl.BlockSpec((tm, tn), lambda i, j, k: (i, j)),
                      pl.BlockSpec((None, tm, LANE), lambda i, j, k: (j, i, 0))]
    res_out = pl.pallas_call(
        functools.partial(_mm_res_kernel, scale=scale, nk=nk, emit=emit, n_res=n_res,
                          n_prompt_tiles=npt),
        out_shape=out_shape,
        grid=(t // tm, n // tn, nk),
        in_specs=in_specs,
        out_specs=out_specs,
        compiler_params=_cparams("arbitrary", "arbitrary", "arbitrary"),
        name="mm_res",
    )(*args)
    return res_out if emit else res_out[0]


def _mm_res_groups_kernel(*refs, n_parts, n_prompt_tiles, project):
    ap = refs[0:n_parts]
    asm = refs[n_parts:2 * n_parts]
    if project:
        w_ref, wp_ref, r_ref, g_ref, o_ref, y_ref = refs[2 * n_parts:]
    else:
        w_ref, r_ref, g_ref, o_ref, xg_ref, ssq_ref = refs[2 * n_parts:]
    i = pl.program_id(0)

    def run(parts):
        acc = r_ref[...]
        k0 = 0
        for p in parts:
            kw = p.shape[1]
            acc = acc + jnp.dot(p[...], w_ref[k0:k0 + kw, :], preferred_element_type=F32)
            k0 += kw
        o_ref[...] = acc
        if project:
            rstd = lax.rsqrt(jnp.mean(acc * acc, axis=-1, keepdims=True) + EPS)
            y = jnp.dot((acc * g_ref[...]).astype(BF16), wp_ref[...], preferred_element_type=F32)
            y_ref[...] = (y * rstd).astype(y_ref.dtype)
        else:
            _emit_norm_inputs(acc, g_ref, xg_ref, ssq_ref)

    @pl.when(i < n_prompt_tiles)
    def _():
        run(ap)

    @pl.when(i >= n_prompt_tiles)
    def _():
        run(asm)


def _mm_res_groups(parts, w, l, res, g_next, w_post=None):
    t, n = res.shape
    p_rows = parts[0][0].shape[0]
    npt = p_rows // TR
    assert p_rows % TR == 0 and (t - p_rows) % TR == 0
    kdim = w.shape[1]
    gains, ln = g_next
    project = w_post is not None
    row_spec = pl.BlockSpec((TR, n), lambda i: (i, 0))
    in_specs = ([pl.BlockSpec((TR, p[0].shape[1]), lambda i: (jnp.minimum(i, npt - 1), 0))
                 for p in parts]
                + [pl.BlockSpec((TR, p[1].shape[1]), lambda i: (jnp.maximum(i - npt, 0), 0))
                   for p in parts]
                + [_layer_spec((kdim, n), l)]
                + ([pl.BlockSpec(w_post.shape, lambda i: (0, 0))] if project else [])
                + [row_spec, _layer_spec((1, n), ln)])
    if project:
        out_shape = [jax.ShapeDtypeStruct((t, n), F32),
                     jax.ShapeDtypeStruct((t, w_post.shape[1]), BF16)]
        out_specs = [row_spec, pl.BlockSpec((TR, w_post.shape[1]), lambda i: (i, 0))]
    else:
        out_shape = [jax.ShapeDtypeStruct((t, n), F32), jax.ShapeDtypeStruct((t, n), BF16),
                     jax.ShapeDtypeStruct((1, t, LANE), F32)]
        out_specs = [row_spec, row_spec, pl.BlockSpec((None, TR, LANE), lambda i: (0, i, 0))]
    return pl.pallas_call(
        functools.partial(_mm_res_groups_kernel, n_parts=len(parts), n_prompt_tiles=npt,
                          project=project),
        out_shape=out_shape,
        grid=(t // TR,),
        in_specs=in_specs,
        out_specs=out_specs,
        compiler_params=_cparams("arbitrary"),
        name="mm_res_groups",
    )(*[p[0] for p in parts], *[p[1] for p in parts], w, *([w_post] if project else []), res,
      gains)


def _pool_gmlp_kernel(xp_ref, u_ref, v_ref, hist_ref, pw_ref, ps_ref, vg_ref, ws_ref, bs_ref,
                      mix_ref, pst_ref, *rest, rows, pos0, want_v, nsq):
    if want_v:
        vrows_ref, full_ref, vpad_ref = rest
    else:
        full_ref, vpad_ref = rest
    r = pl.program_id(1)
    cw = xp_ref.shape[1]
    ng = len(POOL_WINDOWS)
    gd = cw // ng
    nh = ws_ref.shape[0]
    hd = v_ref.shape[1] // nh
    ck = ws_ref.shape[1]
    lc = min(rows, ck)
    tri = (lax.broadcasted_iota(jnp.int32, (lc, ck), 1)
           <= lax.broadcasted_iota(jnp.int32, (lc, ck), 0))
    pos = pos0 + r * rows + lax.broadcasted_iota(jnp.int32, (rows, gd), 0)
    if lc < ck:
        @pl.when(jnp.logical_and(pl.program_id(0) == 0, r == 0))
        def _():
            vpad_ref[...] = jnp.zeros_like(vpad_ref)

    for sq in range(nsq):
        ro = sq * rows
        full = full_ref.at[sq]

        @pl.when(r == 0)
        def _():
            full[0:POOL_PAD, :] = hist_ref[sq]

        @pl.when(r > 0)
        def _():
            full[0:POOL_PAD, :] = full[rows:rows + POOL_PAD, :]

        x = xp_ref[ro:ro + rows, :]
        full[POOL_PAD:POOL_PAD + rows, :] = x
        for g, w in enumerate(POOL_WINDOWS):
            cs = slice(g * gd, (g + 1) * gd)
            acc = x[:, cs]
            for s in range(1, w):
                acc = acc + full[POOL_PAD - s:POOL_PAD - s + rows, cs]
            cnt = jnp.minimum(pos + 1, w).astype(F32)
            dg = acc / cnt - x[:, cs]
            y = jnp.dot(dg.astype(BF16), pw_ref[g], preferred_element_type=F32)
            mix_ref[ro:ro + rows, cs] = (y * ps_ref[:, cs]).astype(BF16)
        pst_ref[sq] = full[rows:rows + POOL_PAD, :]

        for h in range(nh):
            hs = slice(h * hd, (h + 1) * hd)
            wsm = jnp.where(tri, ws_ref[h, 0:lc, :], 0.0).astype(BF16)
            bias = bs_ref[0:lc, hs]
            for c in range(rows // lc):
                rs = slice(ro + c * lc, ro + (c + 1) * lc)
                vh = _rms(v_ref[rs, hs], vg_ref[:, hs])
                if want_v:
                    vrows_ref[rs, hs] = vh
                if lc < ck:
                    vpad_ref[sq * nh + h, 0:lc, :] = vh.astype(BF16)
                    rhs = vpad_ref[sq * nh + h]
                else:
                    rhs = vh.astype(BF16)
                s = jnp.dot(wsm, rhs, preferred_element_type=F32) + bias
                mix_ref[rs, cw + h * hd:cw + (h + 1) * hd] = (u_ref[rs, hs] * s).astype(BF16)


def _pool_gmlp(proj, hist, lh, pool_w, pool_scale, v_gain, ws, bs_full, l, *, row0, nseq, seqlen,
               rows, pos0, want_v, nsq=1):
    cw = pool_w.shape[1] * pool_w.shape[2]
    gw = v_gain.shape[2]
    nt = seqlen // rows
    assert nseq % nsq == 0 and (nsq == 1 or nt == 1)
    blk = nsq * rows
    rb0 = row0 // blk

    def rowmap(off):
        return lambda b, r: (rb0 + b * nt + r, off)

    out_shape = [jax.ShapeDtypeStruct((nseq * seqlen, cw + gw), BF16),
                 jax.ShapeDtypeStruct((nseq, POOL_PAD, cw), F32)]
    out_specs = [pl.BlockSpec((blk, cw + gw), lambda b, r: (b * nt + r, 0)),
                 pl.BlockSpec((nsq, POOL_PAD, cw), lambda b, r: (b, 0, 0))]
    if want_v:
        out_shape.append(jax.ShapeDtypeStruct((nseq * seqlen, gw), F32))
        out_specs.append(pl.BlockSpec((blk, gw), lambda b, r: (b * nt + r, 0)))
    return pl.pallas_call(
        functools.partial(_pool_gmlp_kernel, rows=rows, pos0=pos0, want_v=want_v, nsq=nsq),
        out_shape=out_shape,
        grid=(nseq // nsq, nt),
        in_specs=[
            pl.BlockSpec((blk, cw), rowmap(0)),
            pl.BlockSpec((blk, gw), rowmap(1)),
            pl.BlockSpec((blk, gw), rowmap(2)),
            pl.BlockSpec((None, nsq, POOL_PAD, cw), lambda b, r: (lh, b, 0, 0)),
            _layer_spec(pool_w.shape[1:], l),
            _layer_spec(pool_scale.shape[1:], l),
            _layer_spec(v_gain.shape[1:], l),
            _layer_spec(ws.shape[1:], l),
            _layer_spec(bs_full.shape[1:], l),
        ],
        out_specs=out_specs,
        scratch_shapes=[pltpu.VMEM((nsq, rows + 2 * POOL_PAD, cw), F32),
                        pltpu.VMEM((nsq * ws.shape[1], ws.shape[2], gw // ws.shape[1]), BF16)],
        compiler_params=_cparams("arbitrary", "arbitrary"),
        name="pool_gmlp",
    )(proj, proj, proj, hist, pool_w, pool_scale, v_gain, ws, bs_full)


def _split3(x):
    hi = x.astype(BF16)
    r1 = x - hi.astype(F32)
    mid = r1.astype(BF16)
    lo = (r1 - mid.astype(F32)).astype(BF16)
    return hi, mid, lo


_NT = (((1,), (1,)), ((), ()))
_TN = (((0,), (0,)), ((), ()))


def _mlstm_kernel(q_ref, k_ref, v_ref, o_ref, gt_ref, gb_ref, ng_ref, c0_ref, n0_ref, m0_ref,
                  mix_ref, c_out, n_out, m_out, st_s, m_s, *pads, lq, nchunk, nsq):
    ck = LANE
    nh, dh = st_s.shape[1], st_s.shape[2]
    c = pl.program_id(1)
    eye = (lax.broadcasted_iota(jnp.int32, (dh, dh), 0)
           == lax.broadcasted_iota(jnp.int32, (dh, dh), 1))
    tri = (lax.broadcasted_iota(jnp.int32, (ck, ck), 1)
           <= lax.broadcasted_iota(jnp.int32, (ck, ck), 0))
    rowi = lax.broadcasted_iota(jnp.int32, (ck, LANE), 0)
    kscale = dh ** -0.5
    e_r = lax.broadcasted_iota(jnp.int32, (LANE, nh * dh), 0)
    e_c = lax.broadcasted_iota(jnp.int32, (LANE, nh * dh), 1)
    spread = jnp.where(e_r == lax.shift_right_logical(e_c, dh.bit_length() - 1), 1.0, 0.0)
    spread = spread.astype(BF16)
    tri_b = jnp.where(tri, 1.0, 0.0).astype(BF16)
    ones_k = jnp.ones((ck, dh), BF16)
    ones_d = jnp.ones((dh, dh), BF16)

    @pl.when(c == 0)
    def _():
        for sq in range(nsq):
            for h in range(nh):
                st_s[sq, h, :, 0:dh] = c0_ref[sq, h]
                ncol = jnp.sum(jnp.where(eye, n0_ref[sq, h:h + 1, :], 0.0), axis=1, keepdims=True)
                st_s[sq, h, :, dh:2 * dh] = jnp.broadcast_to(ncol, (dh, dh))
        m_s[...] = m0_ref[...]

    if lq < ck:
        kp_ref, vp_ref, gp_ref = pads

        @pl.when(jnp.logical_and(pl.program_id(0) == 0, c == 0))
        def _():
            kp_ref[...] = jnp.zeros_like(kp_ref)
            vp_ref[...] = jnp.zeros_like(vp_ref)
            gp_ref[...] = jnp.zeros_like(gp_ref)

    seqs = range(nsq)
    rqs = [slice(sq * lq, (sq + 1) * lq) for sq in seqs]
    if lq < ck:
        for sq in seqs:
            kp_ref[sq, 0:lq, :] = k_ref[rqs[sq], :]
            vp_ref[sq, 0:lq, :] = v_ref[rqs[sq], :]
            gp_ref[sq, 0:lq, :] = gt_ref[rqs[sq], :]
        k_src = [kp_ref.at[sq] for sq in seqs]
        v_src = [vp_ref.at[sq] for sq in seqs]
        g_src = [gp_ref.at[sq] for sq in seqs]
    else:
        k_src, v_src, g_src = [k_ref], [v_ref], [gt_ref]

    gts = [g_src[sq][...] + gb_ref[...] for sq in seqs]
    lf = [jax.nn.log_sigmoid(g[:, LANE:2 * LANE]) for g in gts]
    b_all = [sum(jnp.dot(tri_b, p, preferred_element_type=F32) for p in _split3(x)) for x in lf]
    r_all = [gts[sq][:, 0:LANE] - b_all[sq] for sq in seqs]
    pm = list(r_all)
    sh = 1
    while sh < ck:
        pm = [jnp.maximum(x, jnp.where(rowi >= sh, pltpu.roll(x, sh, axis=0), NEG_BIG)) for x in pm]
        sh *= 2
    m_all = [m_s[sq] for sq in seqs]
    big_m = [jnp.maximum(m_all[sq], pm[sq]) for sq in seqs]
    neg_m = [jnp.log(kscale) - x for x in big_m]
    iw_all = [jnp.exp(m_all[sq] - big_m[sq]) for sq in seqs]
    e_all = [jnp.exp(-(b_all[sq] + big_m[sq])) for sq in seqs]
    m_last = [x[lq - 1:lq, :] for x in big_m]
    for sq in seqs:
        m_s[sq] = b_all[sq][lq - 1:lq, :] + m_last[sq]
    decay_all = [jnp.exp(m_all[sq] - m_last[sq]) for sq in seqs]
    wt_all = [jnp.where(rowi < lq, jnp.exp(r_all[sq] - m_last[sq]), 0.0) * kscale for sq in seqs]
    iw_bc = [jnp.dot(x.astype(BF16), spread, preferred_element_type=F32) for x in iw_all]
    wt_bc = [jnp.dot(x.astype(BF16), spread, preferred_element_type=F32) for x in wt_all]
    r_rows = [x.T for x in r_all]

    units = [(sq, h, slice(h * dh, (h + 1) * dh)) for sq in seqs for h in range(nh)]
    s_raw = [lax.dot_general(q_ref[rqs[sq], hs], k_src[sq][:, hs], _NT, preferred_element_type=F32)
             for sq, h, hs in units]
    sts = [st_s[sq, h] for sq, h, hs in units]
    lhs = []
    for u, (sq, h, hs) in enumerate(units):
        dm = neg_m[sq][0:lq, h:h + 1] + r_rows[sq][h:h + 1, :]
        dexp = jnp.exp(jnp.where(tri[0:lq, :], dm, NEG_BIG))
        qi = (iw_bc[sq][0:lq, hs] * q_ref[rqs[sq], hs].astype(F32)).astype(BF16)
        lhs.append(jnp.concatenate([(s_raw[u] * dexp).astype(BF16), qi], axis=1))
    nd = [jnp.dot(lhs[u], jnp.concatenate([jnp.concatenate([v_src[sq][:, hs], ones_k], axis=1),
                                           sts[u].astype(BF16)], axis=0),
                  preferred_element_type=F32) for u, (sq, h, hs) in enumerate(units)]
    hv = [nd[u][:, 0:dh] / jnp.maximum(jnp.abs(nd[u][:, dh:2 * dh]), e_all[sq][0:lq, h:h + 1])
          for u, (sq, h, hs) in enumerate(units)]
    ms = [jnp.dot((x * x).astype(BF16), ones_d, preferred_element_type=F32) * (1.0 / dh) for x in hv]
    for u, (sq, h, hs) in enumerate(units):
        hn = hv[u] * lax.rsqrt(ms[u] + EPS) * ng_ref[:, hs]
        mix_ref[rqs[sq], hs] = (hn * jax.nn.sigmoid(o_ref[rqs[sq], hs].astype(F32))).astype(BF16)
    for u, (sq, h, hs) in enumerate(units):
        wtb = wt_bc[sq][:, hs]
        waug = jnp.concatenate([(wtb * v_src[sq][:, hs].astype(F32)).astype(BF16),
                                wtb.astype(BF16)], axis=1)
        st_s[sq, h] = (decay_all[sq][:, h:h + 1] * sts[u]
                       + lax.dot_general(k_src[sq][:, hs], waug, _TN, preferred_element_type=F32))

    @pl.when(c == nchunk - 1)
    def _():
        for sq in range(nsq):
            for h in range(nh):
                c_out[sq, h] = st_s[sq, h, :, 0:dh]
                n_out[sq, h:h + 1, :] = jnp.sum(jnp.where(eye, st_s[sq, h, :, dh:2 * dh], 0.0),
                                                axis=0, keepdims=True)
        m_out[...] = m_s[...]


def _mlstm(proj, gates, gbias, ngain, l, c0, n0, m0, ls, *, row0, nseq, seqlen):
    nh, dh = c0.shape[2], c0.shape[3]
    w = nh * dh
    lq = min(seqlen, LANE)
    nchunk = seqlen // lq
    nsq = MLSTM_SEQS if nchunk == 1 else 1
    assert nseq % nsq == 0
    rows = nsq * lq
    rb0 = row0 // rows

    def rowmap(off):
        return lambda b, c: (rb0 + b * nchunk + c, off)

    pads = []
    if lq < LANE:
        pads = [pltpu.VMEM((nsq, LANE, w), BF16), pltpu.VMEM((nsq, LANE, w), BF16),
                pltpu.VMEM((nsq, LANE, 2 * LANE), F32)]
    return pl.pallas_call(
        functools.partial(_mlstm_kernel, lq=lq, nchunk=nchunk, nsq=nsq),
        out_shape=[jax.ShapeDtypeStruct((nseq * seqlen, w), BF16),
                   jax.ShapeDtypeStruct((nseq, nh, dh, dh), F32),
                   jax.ShapeDtypeStruct((nseq, nh, dh), F32),
                   jax.ShapeDtypeStruct((nseq, 1, LANE), F32)],
        grid=(nseq // nsq, nchunk),
        in_specs=[
            pl.BlockSpec((rows, w), rowmap(0)),
            pl.BlockSpec((rows, w), rowmap(1)),
            pl.BlockSpec((rows, w), rowmap(2)),
            pl.BlockSpec((rows, w), rowmap(3)),
            pl.BlockSpec((rows, 2 * LANE), rowmap(0)),
            _layer_spec((1, 2 * LANE), l),
            _layer_spec((1, w), l),
            pl.BlockSpec((None, nsq, nh, dh, dh), lambda b, c: (ls, b, 0, 0, 0)),
            pl.BlockSpec((None, nsq, nh, dh), lambda b, c: (ls, b, 0, 0)),
            pl.BlockSpec((None, nsq, 1, LANE), lambda b, c: (ls, b, 0, 0)),
        ],
        out_specs=[
            pl.BlockSpec((rows, w), lambda b, c: (b * nchunk + c, 0)),
            pl.BlockSpec((nsq, nh, dh, dh), lambda b, c: (b, 0, 0, 0)),
            pl.BlockSpec((nsq, nh, dh), lambda b, c: (b, 0, 0)),
            pl.BlockSpec((nsq, 1, LANE), lambda b, c: (b, 0, 0)),
        ],
        scratch_shapes=[pltpu.VMEM((nsq, nh, dh, 2 * dh), F32),
                        pltpu.VMEM((nsq, 1, LANE), F32)] + pads,
        compiler_params=_cparams("arbitrary", "arbitrary"),
        name="mlstm",
    )(proj, proj, proj, proj, gates, gbias, ngain, c0, n0, m0)


def _attn_kernel(q_ref, k_ref, v_ref, o_ref):
    nsq, d = k_ref.shape[0], q_ref.shape[1]
    rows = q_ref.shape[0] // nsq
    hd = d // MEM_HEADS
    scale = hd ** -0.5
    units = [(sq, slice(sq * rows, (sq + 1) * rows), slice(h * hd, (h + 1) * hd))
             for sq in range(nsq) for h in range(MEM_HEADS)]
    kb = [k_ref[sq, :, hs].astype(BF16) for sq, rs, hs in units]
    vb = [v_ref[sq, :, hs].astype(BF16) for sq, rs, hs in units]
    s = [lax.dot_general(q_ref[rs, hs], kb[u], _NT, preferred_element_type=F32) * scale
         for u, (sq, rs, hs) in enumerate(units)]
    p = [jnp.exp(x - jnp.max(x, axis=-1, keepdims=True)) for x in s]
    p = [x / jnp.sum(x, axis=-1, keepdims=True) for x in p]
    for u, (sq, rs, hs) in enumerate(units):
        o_ref[rs, hs] = jnp.dot(p[u].astype(BF16), vb[u], preferred_element_type=F32).astype(BF16)


def _attn(q, mem_k, mem_v, l, *, row0, nseq, seqlen, rows, nsq=1):
    d = q.shape[1]
    nm = mem_k.shape[2]
    nt = seqlen // rows
    assert nseq % nsq == 0 and (nsq == 1 or nt == 1)
    blk = nsq * rows
    rb0 = row0 // blk
    kv_spec = pl.BlockSpec((None, nsq, nm, d), lambda b, r: (l, b, 0, 0))
    return pl.pallas_call(
        _attn_kernel,
        out_shape=jax.ShapeDtypeStruct((nseq * seqlen, d), BF16),
        grid=(nseq // nsq, nt),
        in_specs=[pl.BlockSpec((blk, d), lambda b, r: (rb0 + b * nt + r, 0)), kv_spec, kv_spec],
        out_specs=pl.BlockSpec((blk, d), lambda b, r: (b * nt + r, 0)),
        compiler_params=_cparams("parallel", "arbitrary"),
        name="mem_attn",
    )(q, mem_k, mem_v)


def _final_norm_kernel(x_ref, g_ref, o_ref):
    o_ref[...] = _rms(x_ref[...], g_ref[...])


def _final_norm(x, g, *, row0, nrows, rows=512):
    d = x.shape[1]
    rb0 = row0 // rows
    return pl.pallas_call(
        _final_norm_kernel,
        out_shape=jax.ShapeDtypeStruct((nrows, d), F32),
        grid=(nrows // rows,),
        in_specs=[pl.BlockSpec((rows, d), lambda i: (rb0 + i, 0)),
                  pl.BlockSpec((1, d), lambda i: (0, 0))],
        out_specs=pl.BlockSpec((rows, d), lambda i: (i, 0)),
        compiler_params=_cparams("parallel"),
        name="final_norm",
    )(x, g)


def kernel(x_prompt, x_sample, mem_prompt, state_pool, state_mlstm_C, state_mlstm_n, state_mlstm_m, cache_mem_k, cache_mem_v, g_ffn1, w_ffn1_in, w_ffn1_out, g_mix, w_in, pool_w, pool_scale, gmlp_v_gain, gmlp_ws, gmlp_bs, mlstm_i_bias, mlstm_f_bias, mlstm_norm_gain, w_out, g_xattn, g_mem, w_mem_q, w_mem_k, w_mem_v, w_mem_o, g_ffn2, w_ffn2_in, w_ffn2_out, g_final):
    nb, seq, d = x_prompt.shape
    db, dseq, _ = x_sample.shape
    depth = g_ffn1.shape[0]
    nh, dh = state_mlstm_C.shape[2], state_mlstm_C.shape[3]
    mw = nh * dh
    pw = pool_w.shape[1] * pool_w.shape[2]
    gh = gmlp_ws.shape[1]
    gw = gmlp_v_gain.shape[1]
    ghd = gw // gh
    nmem = mem_prompt.shape[1]
    npool = state_pool.shape[2]
    p_rows = nb * seq
    s_rows = db * dseq
    past = seq
    mix_cols = pw + 2 * gw
    main_cols = mix_cols + 4 * mw

    rows3 = lambda v: v.reshape(depth, 1, -1)
    g1, gm, gx, g2 = rows3(g_ffn1), rows3(g_mix), rows3(g_xattn), rows3(g_ffn2)

    w_in_b = w_in.astype(BF16)
    zpad = jnp.zeros((depth, d, LANE - nh), BF16)
    w_gate = jnp.concatenate([w_in_b[:, :, main_cols:main_cols + nh], zpad,
                              w_in_b[:, :, main_cols + nh:], zpad], axis=2)
    bpad = jnp.zeros((depth, LANE - nh), F32)
    gbias = rows3(jnp.concatenate([mlstm_i_bias, bpad, mlstm_f_bias, bpad], axis=1))
    pool_w_b = pool_w.astype(BF16)
    bs_full = jnp.repeat(jnp.swapaxes(gmlp_bs, 1, 2), ghd, axis=2)
    hist_p = jnp.zeros((1, nb, POOL_PAD, pw), F32)
    hist_s = jnp.pad(state_pool, ((0, 0), (0, 0), (POOL_PAD - npool, 0), (0, 0)))
    c0_p = jnp.zeros((1, nb, nh, dh, dh), F32)
    n0_p = jnp.zeros((1, nb, nh, dh), F32)
    m0_p = jnp.zeros((1, nb, 1, LANE), F32)
    c0_s = state_mlstm_C.astype(F32)
    n0_s = state_mlstm_n.astype(F32)
    m0_s = jnp.pad(state_mlstm_m.astype(F32), ((0, 0), (0, 0), (0, LANE - nh)))[:, :, None, :]

    mk, mv = _mem_kv(mem_prompt.reshape(nb * nmem, d), rows3(g_mem), w_mem_k, w_mem_v)
    mk = mk.reshape(depth, nb, nmem, d)
    mv = mv.reshape(depth, nb, nmem, d)

    x = (x_prompt.reshape(p_rows, d), x_sample.reshape(s_rows, d))
    xg, ssq = _prep(*x, g1, 0)

    outs = {k: [] for k in ("pool_p", "pool_s", "c_p", "n_p", "m_p", "c_s", "n_s", "m_s", "v_s")}
    mm_tiles = dict(tm=1536, tk=2048)
    w_next = None
    for l in range(depth):
        w_up, lw = (w_ffn1_in, l) if w_next is None else (w_next[None], 0)
        act, w_down, w_out_b, wq_b, wo_b, w_next = _ffn_in(
            xg, ssq, w_up, lw,
            [(w_ffn1_out, l), (w_out, l), (w_mem_q, l), (w_mem_o, l), (w_ffn2_in, l)])
        x, xg, ssq = _mm_res(act, w_down, x, scale=0.5, g_next=(gm, l), **(mm_tiles if l else {}))

        proj_main, proj_mix, gates = _mix_in(xg, ssq, w_in_b, w_gate, l, n_f32_cols=mix_cols,
                                             n_cols=main_cols)

        common = (pool_w_b, rows3(pool_scale), rows3(gmlp_v_gain), gmlp_ws, bs_full, l)
        mix_a_p, pst_p = _pool_gmlp(proj_mix, hist_p, 0, *common, row0=0, nseq=nb, seqlen=seq,
                                    rows=1024, pos0=0, want_v=False)
        mix_a_s, pst_s, v_rows = _pool_gmlp(proj_mix, hist_s, l, *common, row0=p_rows, nseq=db,
                                            seqlen=dseq, rows=dseq, pos0=past, want_v=True,
                                            nsq=SHORT_SEQS)
        outs["pool_p"].append(pst_p[:, POOL_PAD - npool:])
        outs["pool_s"].append(pst_s[:, POOL_PAD - npool:])
        outs["v_s"].append(v_rows.reshape(db, dseq, gw))

        ngain = rows3(mlstm_norm_gain)
        mix_b_p, c_p, n_p, m_p = _mlstm(proj_main, gates, gbias, ngain, l, c0_p, n0_p, m0_p, 0,
                                        row0=0, nseq=nb, seqlen=seq)
        mix_b_s, c_s, n_s, m_s = _mlstm(proj_main, gates, gbias, ngain, l, c0_s, n0_s, m0_s, l,
                                        row0=p_rows, nseq=db, seqlen=dseq)
        outs["c_p"].append(c_p)
        outs["n_p"].append(n_p)
        outs["m_p"].append(m_p[:, 0, :nh])
        outs["c_s"].append(c_s)
        outs["n_s"].append(n_s)
        outs["m_s"].append(m_s[:, 0, :nh])

        x, q = _mm_res_groups([(mix_a_p, mix_a_s), (mix_b_p, mix_b_s)], w_out_b[None], 0, x,
                              (gx, l), w_post=wq_b)

        att_p = _attn(q, mk, mv, l, row0=0, nseq=nb, seqlen=seq, rows=1024)
        att_s = _attn(q, cache_mem_k, cache_mem_v, l, row0=p_rows, nseq=db, seqlen=dseq,
                      rows=dseq, nsq=MLSTM_SEQS)
        x, xg, ssq = _mm_res_groups([(att_p, att_s)], wo_b[None], 0, x, (g2, l))

        if l + 1 < depth:
            act, w_down, w_next = _ffn_in(xg, ssq, w_next[None], 0,
                                          [(w_ffn2_out, l), (w_ffn1_in, l + 1)])
            x, xg, ssq = _mm_res(act, w_down, x, scale=0.5, g_next=(g1, l + 1), **mm_tiles)
        else:
            act, w_down = _ffn_in(xg, ssq, w_next[None], 0, [(w_ffn2_out, l)])
            x = _mm_res(act, w_down, x, scale=0.5, **mm_tiles)

    gfin = g_final.reshape(1, d)
    y_prompt = _final_norm(x, gfin, row0=0, nrows=p_rows).reshape(nb, seq, d)
    y_sample = _final_norm(x, gfin, row0=p_rows, nrows=s_rows).reshape(db, dseq, d)
    st = lambda k: jnp.stack(outs[k])
    return (y_prompt, y_sample, st("pool_p"), st("pool_s"), st("c_p"), st("n_p"), st("m_p"),
            st("c_s"), st("n_s"), st("m_s"), st("v_s"), mk, mv)
```
